```python
import jax, jax.numpy as jnp
from jax import lax
import numpy as np

D_MODEL = 2048
BATCH = 4
SEQ = 2048
DEPTH = 2
DEC_BATCH = 128
DEC_SEQ = 1
PAST_LEN = 16384
PAGE_SIZE = 128

MLA_HEADS = 8
MLA_Q_LORA = 512
MLA_KV_LORA = 256
MLA_NOPE = 128
MLA_ROPE = 64
MLA_V = 128
FOX_HEADS = 8
FOX_KV_HEADS = 2
FOX_DIM = 128
FORGET_BIAS = 2.0
ROPE_THETA = 10000.0
Q_BLOCK = 128
GDN_HEADS = 16
GDN_DK = 128
GDN_DV = 128
GDN_CONV = 4
GDN_CHUNK = 64
D_FF = 5632
N_EXPERTS = 8
TOP_K = 2
D_FF_EXPERT = 2816
N_MOD = 6
EPS = 1e-6

MLA_SCALE = (MLA_NOPE + MLA_ROPE) ** -0.5
FOX_SCALE = FOX_DIM ** -0.5
EVEN_SPLITS = (MLA_Q_LORA, MLA_KV_LORA, MLA_ROPE, FOX_HEADS * FOX_DIM, FOX_KV_HEADS * FOX_DIM, FOX_KV_HEADS * FOX_DIM, FOX_HEADS)
EVEN_IN = sum(EVEN_SPLITS)
EVEN_MIX = MLA_HEADS * MLA_V + FOX_HEADS * FOX_DIM
GDN_QK = GDN_HEADS * GDN_DK
GDN_VW = GDN_HEADS * GDN_DV
GDN_CONV_CH = 2 * GDN_QK + GDN_VW
ODD_SPLITS = (GDN_CONV_CH, GDN_VW, GDN_HEADS, GDN_HEADS)
ODD_IN = sum(ODD_SPLITS)

kernel_name = 'hybrid_mla_fox_gdn_moe_adaln_step'


def split_cols(x, sizes):
    out, start = [], 0
    for s in sizes:
        out.append(x[..., start:start + s])
        start += s
    return out


def rmsnorm(x, gain):
    xf = x.astype(jnp.float32)
    y = xf * lax.rsqrt(jnp.mean(xf * xf, axis=-1, keepdims=True) + EPS)
    return (y * gain.astype(jnp.float32)).astype(x.dtype)


def l2norm(x):
    xf = x.astype(jnp.float32)
    return xf * lax.rsqrt(jnp.sum(xf * xf, axis=-1, keepdims=True) + EPS)


def rope(x, pos):
    half = x.shape[-1] // 2
    inv = ROPE_THETA ** (-jnp.arange(half, dtype=jnp.float32) / half)
    ang = pos.astype(jnp.float32)[:, None] * inv[None, :]
    ang = ang.reshape((1, pos.shape[0]) + (1,) * (x.ndim - 3) + (half,))
    cos, sin = jnp.cos(ang), jnp.sin(ang)
    xf = x.astype(jnp.float32)
    x1, x2 = xf[..., :half], xf[..., half:]
    return jnp.concatenate([x1 * cos - x2 * sin, x1 * sin + x2 * cos], axis=-1).astype(x.dtype)


def adaln(c, w_mod, b_mod):
    m = jax.nn.silu(c) @ w_mod + b_mod
    return jnp.split(m[:, None, :], N_MOD, axis=-1)


def modulate(x, gain, shift, scale):
    return rmsnorm(x, gain) * (1.0 + scale) + shift


def swiglu(h, w1, w3, w2):
    return (jax.nn.silu(h @ w1) * (h @ w3)) @ w2


def moe_swiglu(h, w_router, b_router, w_gate, w_up, w_down):
    logits = (h @ w_router).astype(jnp.float32) + b_router.astype(jnp.float32)
    top_val, top_idx = lax.top_k(logits, TOP_K)
    top_w = jax.nn.softmax(top_val, axis=-1)
    gates = jnp.sum(jax.nn.one_hot(top_idx, N_EXPERTS, dtype=jnp.float32) * top_w[..., None], axis=-2)
    out = jnp.zeros(h.shape[:-1] + (w_down.shape[-1],), jnp.float32)
    for e in range(N_EXPERTS):
        y_e = (jax.nn.silu(h @ w_gate[e]) * (h @ w_up[e])) @ w_down[e]
        out = out + gates[..., e:e + 1] * y_e.astype(jnp.float32)
    return out.astype(h.dtype)


def mla_fox_project(h, pos, w_in, g_qa, w_uq, g_kva, b_f):
    B, T, _ = h.shape
    q_lat, kv_lat, k_r, fq, fk, fv, f_logit = split_cols(h @ w_in, EVEN_SPLITS)
    q = (rmsnorm(q_lat, g_qa) @ w_uq).reshape(B, T, MLA_HEADS, MLA_NOPE + MLA_ROPE)
    q_nope = q[..., :MLA_NOPE]
    q_rope = rope(q[..., MLA_NOPE:], pos)
    c_kv = rmsnorm(kv_lat, g_kva)
    k_rope = rope(k_r, pos)
    fq = fq.reshape(B, T, FOX_HEADS, FOX_DIM)
    fk = fk.reshape(B, T, FOX_KV_HEADS, FOX_DIM)
    fv = fv.reshape(B, T, FOX_KV_HEADS, FOX_DIM)
    lf = jax.nn.log_sigmoid(f_logit.astype(jnp.float32) + b_f.astype(jnp.float32))
    return q_nope, q_rope, c_kv, k_rope, fq, fk, fv, lf


def block_causal_attention(q, k, v, scale, fcum=None):
    B, S, H, D = q.shape
    Hk = k.shape[2]
    G = H // Hk
    nb = S // Q_BLOCK
    qb = q.reshape(B, nb, Q_BLOCK, Hk, G, D).swapaxes(0, 1)
    kpos = jnp.arange(S)
    if fcum is None:
        xs = (jnp.arange(nb), qb)
    else:
        fkey = fcum.reshape(B, S, Hk, G).transpose(0, 2, 3, 1)
        xs = (jnp.arange(nb), qb, fcum.reshape(B, nb, Q_BLOCK, Hk, G).swapaxes(0, 1))

    def one(args):
        i, q_blk = args[0], args[1]
        s = jnp.einsum('bqkgd,bskd->bkgqs', q_blk, k).astype(jnp.float32) * scale
        if fcum is not None:
            s = s + args[2].transpose(0, 2, 3, 1)[..., None] - fkey[:, :, :, None, :]
        qpos = i * Q_BLOCK + jnp.arange(Q_BLOCK)
        s = jnp.where(qpos[:, None] >= kpos[None, :], s, -jnp.inf)
        p = jax.nn.softmax(s, axis=-1).astype(v.dtype)
        return jnp.einsum('bkgqs,bskd->bqkgd', p, v)

    o = lax.map(one, xs)
    return o.swapaxes(0, 1).reshape(B, S, H, v.shape[-1])


def paged_rows(pool, pages):
    g = pool[pages]
    return g.reshape((-1,) + pool.shape[2:])


def decode_mask(past, t):
    return jnp.arange(past + t)[None, :] <= (past + jnp.arange(t))[:, None]


def mla_decode(q_abs, q_rope, c_new, r_new, page_table, cache_ckv, cache_krope):
    T = q_abs.shape[1]
    past = page_table.shape[1] * cache_ckv.shape[1]
    mask = decode_mask(past, T)

    def one(args):
        pt, ql, qr, cn, rn = args
        c_all = jnp.concatenate([paged_rows(cache_ckv, pt).astype(cn.dtype), cn], axis=0)
        r_all = jnp.concatenate([paged_rows(cache_krope, pt).astype(rn.dtype), rn], axis=0)
        s = (jnp.einsum('thc,sc->hts', ql, c_all) + jnp.einsum('thr,sr->hts', qr, r_all)).astype(jnp.float32) * MLA_SCALE
        s = jnp.where(mask[None], s, -jnp.inf)
        p = jax.nn.softmax(s, axis=-1).astype(c_all.dtype)
        return jnp.einsum('hts,sc->thc', p, c_all)

    return lax.map(one, (page_table, q_abs, q_rope, c_new, r_new))


def fox_decode(q, k_new, v_new, lf_new, page_table, cache_k, cache_v, cache_lf):
    DB, T, H, D = q.shape
    Hk = FOX_KV_HEADS
    G = H // Hk
    past = page_table.shape[1] * cache_k.shape[1]
    mask = decode_mask(past, T)

    def one(args):
        pt, qi, kn, vn, ln = args
        k_all = jnp.concatenate([paged_rows(cache_k, pt).astype(kn.dtype), kn], axis=0)
        v_all = jnp.concatenate([paged_rows(cache_v, pt).astype(vn.dtype), vn], axis=0)
        lf_all = jnp.concatenate([paged_rows(cache_lf, pt).astype(jnp.float32), ln], axis=0)
        f_pos = lf_all - lax.cumsum(lf_all, axis=0, reverse=True)
        f_key = f_pos.reshape(-1, Hk, G).transpose(1, 2, 0)
        f_qry = f_pos[past:].reshape(T, Hk, G).transpose(1, 2, 0)
        s = jnp.einsum('tkgd,skd->kgts', qi.reshape(T, Hk, G, D), k_all).astype(jnp.float32) * FOX_SCALE
        s = s + f_qry[..., None] - f_key[:, :, None, :]
        s = jnp.where(mask[None, None], s, -jnp.inf)
        p = jax.nn.softmax(s, axis=-1).astype(v_all.dtype)
        return jnp.einsum('kgts,skd->tkgd', p, v_all).reshape(T, H, D)

    return lax.map(one, (page_table, q, k_new, v_new, lf_new))


def even_mixer_prompt(h, pos, w_in, g_qa, w_uq, g_kva, w_uk, w_uv, b_f, w_out):
    B, T, _ = h.shape
    q_nope, q_rope, c_kv, k_rope, fq, fk, fv, lf = mla_fox_project(h, pos, w_in, g_qa, w_uq, g_kva, b_f)
    k_nope = jnp.einsum('btc,chd->bthd', c_kv, w_uk)
    v_mla = jnp.einsum('btc,chd->bthd', c_kv, w_uv)
    q_mla = jnp.concatenate([q_nope, q_rope], axis=-1)
    k_mla = jnp.concatenate([k_nope, jnp.broadcast_to(k_rope[:, :, None, :], (B, T, MLA_HEADS, MLA_ROPE))], axis=-1)
    o_mla = block_causal_attention(q_mla, k_mla, v_mla, MLA_SCALE)
    o_fox = block_causal_attention(fq, fk, fv, FOX_SCALE, jnp.cumsum(lf, axis=1))
    o = jnp.concatenate([o_mla.reshape(B, T, -1), o_fox.reshape(B, T, -1).astype(o_mla.dtype)], axis=-1)
    return o @ w_out, (c_kv, k_rope, fk, fv, lf)


def even_mixer_sample(h, pos, page_table, cache_ckv, cache_krope, cache_k, cache_v, cache_lf,
                      w_in, g_qa, w_uq, g_kva, w_uk, w_uv, b_f, w_out):
    B, T, _ = h.shape
    q_nope, q_rope, c_kv, k_rope, fq, fk, fv, lf = mla_fox_project(h, pos, w_in, g_qa, w_uq, g_kva, b_f)
    q_abs = jnp.einsum('bthd,chd->bthc', q_nope, w_uk)
    o_lat = mla_decode(q_abs, q_rope, c_kv, k_rope, page_table, cache_ckv, cache_krope)
    o_mla = jnp.einsum('bthc,chd->bthd', o_lat, w_uv)
    o_fox = fox_decode(fq, fk, fv, lf, page_table, cache_k, cache_v, cache_lf)
    o = jnp.concatenate([o_mla.reshape(B, T, -1), o_fox.reshape(B, T, -1).astype(o_mla.dtype)], axis=-1)
    return o @ w_out, (c_kv, k_rope, fk, fv, lf)


def to_chunks(x, n, c):
    x = x.reshape((x.shape[0], n, c) + x.shape[2:])
    return jnp.transpose(x, (1, 0, 3, 2) + tuple(range(4, x.ndim)))


def gated_delta_chunked(q, k, v, g, beta, s0):
    B, S, H, _ = q.shape
    n = S // GDN_CHUNK
    qc, kc, vc, gc, bc = (to_chunks(t, n, GDN_CHUNK) for t in (q, k, v, g, beta))
    gcum = jnp.cumsum(gc, axis=-1)
    idx = jnp.arange(GDN_CHUNK)
    incl = idx[:, None] >= idx[None, :]
    strict = idx[:, None] > idx[None, :]
    decay = jnp.exp(jnp.where(incl, gcum[..., :, None] - gcum[..., None, :], -jnp.inf))
    a_mat = jnp.where(strict, decay, 0.0) * bc[..., :, None] * jnp.einsum('nbhid,nbhjd->nbhij', kc, kc)
    t_mat = jnp.eye(GDN_CHUNK, dtype=jnp.float32) + a_mat
    w = lax.linalg.triangular_solve(t_mat, (bc * jnp.exp(gcum))[..., None] * kc, left_side=True, lower=True, unit_diagonal=True)
    u = lax.linalg.triangular_solve(t_mat, bc[..., None] * vc, left_side=True, lower=True, unit_diagonal=True)
    qk = jnp.einsum('nbhid,nbhjd->nbhij', qc, kc) * decay
    glast = gcum[..., -1]
    k_end = kc * jnp.exp(glast[..., None] - gcum)[..., None]

    def step(state, xs):
        q_n, w_n, u_n, qk_n, g_n, gl_n, ke_n = xs
        u_corr = u_n - jnp.einsum('bhck,bhkv->bhcv', w_n, state)
        o_n = jnp.exp(g_n)[..., None] * jnp.einsum('bhck,bhkv->bhcv', q_n, state) + jnp.einsum('bhij,bhjv->bhiv', qk_n, u_corr)
        state = jnp.exp(gl_n)[..., None, None] * state + jnp.einsum('bhck,bhcv->bhkv', ke_n, u_corr)
        return state, o_n

    s_fin, o = lax.scan(step, s0, (qc, w, u, qk, gcum, glast, k_end))
    o = jnp.transpose(o, (1, 0, 3, 2, 4)).reshape(B, S, H, -1)
    return o, s_fin


def gated_delta_recurrent(q, k, v, g, beta, s0):
    def step(state, xs):
        q_t, k_t, v_t, g_t, b_t = xs
        state = state * jnp.exp(g_t)[..., None, None]
        u_t = b_t[..., None] * (v_t - jnp.einsum('bhk,bhkv->bhv', k_t, state))
        state = state + k_t[..., :, None] * u_t[..., None, :]
        return state, jnp.einsum('bhk,bhkv->bhv', q_t, state)

    xs = tuple(jnp.swapaxes(t, 0, 1) for t in (q, k, v, g, beta))
    s_fin, o = lax.scan(step, s0, xs)
    return jnp.swapaxes(o, 0, 1), s_fin


def gdn_mixer(h, conv_prev, s0, w_in, conv_w, a_log, dt_bias, g_o, w_out, chunked):
    B, T, _ = h.shape
    qkv, z, a, b = split_cols(h @ w_in, ODD_SPLITS)
    xin = jnp.concatenate([conv_prev.astype(qkv.dtype), qkv], axis=1)
    y = lax.conv_general_dilated(xin, conv_w.astype(xin.dtype), (1,), 'VALID',
                                 dimension_numbers=('NWC', 'WIO', 'NWC'), feature_group_count=GDN_CONV_CH)
    q, k, v = split_cols(jax.nn.silu(y), (GDN_QK, GDN_QK, GDN_VW))
    q = l2norm(q.reshape(B, T, GDN_HEADS, GDN_DK)) * (GDN_DK ** -0.5)
    k = l2norm(k.reshape(B, T, GDN_HEADS, GDN_DK))
    v = v.reshape(B, T, GDN_HEADS, GDN_DV).astype(jnp.float32)
    g = -jnp.exp(a_log.astype(jnp.float32)) * jax.nn.softplus(a.astype(jnp.float32) + dt_bias.astype(jnp.float32))
    beta = jax.nn.sigmoid(b.astype(jnp.float32))
    rule = gated_delta_chunked if chunked else gated_delta_recurrent
    o, s_new = rule(q, k, v, g, beta, s0.astype(jnp.float32))
    o = rmsnorm(o, g_o) * jax.nn.silu(z.reshape(B, T, GDN_HEADS, GDN_DV).astype(jnp.float32))
    out = o.reshape(B, T, GDN_VW).astype(h.dtype) @ w_out
    return out, s_new, xin[:, -(GDN_CONV - 1):]


def setup_inputs(seed: int = 0) -> dict:
    key = jax.random.key(seed)
    ks = iter(jax.random.split(key, 64))
    f32 = jnp.float32
    n_pages = PAST_LEN // PAGE_SIZE
    n_pool = (DEC_BATCH * n_pages * 5) // 4

    def nrm(shape, scale=1.0):
        return jax.random.normal(next(ks), shape, f32) * scale

    def gain(n):
        return 1.0 + nrm((n,), 0.1)

    def lin(fi, fo):
        return nrm((fi, fo), fi ** -0.5)

    d = {}
    d['x_prompt'] = nrm((BATCH, SEQ, D_MODEL))
    d['x_sample'] = nrm((DEC_BATCH, DEC_SEQ, D_MODEL))
    d['cache_mla_ckv'] = nrm((n_pool, PAGE_SIZE, MLA_KV_LORA))
    d['cache_mla_krope'] = nrm((n_pool, PAGE_SIZE, MLA_ROPE))
    d['cache_fox_k'] = nrm((n_pool, PAGE_SIZE, FOX_KV_HEADS, FOX_DIM))
    d['cache_fox_v'] = nrm((n_pool, PAGE_SIZE, FOX_KV_HEADS, FOX_DIM))
    d['cache_fox_logf'] = jax.nn.log_sigmoid(FORGET_BIAS + nrm((n_pool, PAGE_SIZE, FOX_HEADS)))
    d['state_gdn'] = nrm((DEC_BATCH, GDN_HEADS, GDN_DK, GDN_DV), 0.1)
    d['state_gdn_conv'] = nrm((DEC_BATCH, GDN_CONV - 1, GDN_CONV_CH))
    perm = jax.random.permutation(next(ks), n_pool)
    d['page_table'] = perm[:DEC_BATCH * n_pages].reshape(DEC_BATCH, n_pages).astype(jnp.int32)
    d['c_prompt'] = nrm((BATCH, D_MODEL))
    d['c_sample'] = nrm((DEC_BATCH, D_MODEL))
    d['l0_w_mod'] = nrm((D_MODEL, N_MOD * D_MODEL), 0.5 * D_MODEL ** -0.5)
    d['l0_b_mod'] = nrm((N_MOD * D_MODEL,), 0.02)
    d['l0_g_mix'] = gain(D_MODEL)
    d['l0_g_ffn'] = gain(D_MODEL)
    d['l0_w_in'] = lin(D_MODEL, EVEN_IN)
    d['l0_g_qa'] = gain(MLA_Q_LORA)
    d['l0_w_uq'] = lin(MLA_Q_LORA, MLA_HEADS * (MLA_NOPE + MLA_ROPE))
    d['l0_g_kva'] = gain(MLA_KV_LORA)
    d['l0_w_uk'] = nrm((MLA_KV_LORA, MLA_HEADS, MLA_NOPE), MLA_KV_LORA ** -0.5)
    d['l0_w_uv'] = nrm((MLA_KV_LORA, MLA_HEADS, MLA_V), MLA_KV_LORA ** -0.5)
    d['l0_b_f'] = FORGET_BIAS + nrm((FOX_HEADS,), 0.5)
    d['l0_w_out'] = lin(EVEN_MIX, D_MODEL)
    d['l0_w1'] = lin(D_MODEL, D_FF)
    d['l0_w3'] = lin(D_MODEL, D_FF)
    d['l0_w2'] = lin(D_FF, D_MODEL)
    d['l1_w_mod'] = nrm((D_MODEL, N_MOD * D_MODEL), 0.5 * D_MODEL ** -0.5)
    d['l1_b_mod'] = nrm((N_MOD * D_MODEL,), 0.02)
    d['l1_g_mix'] = gain(D_MODEL)
    d['l1_g_ffn'] = gain(D_MODEL)
    d['l1_w_in'] = lin(D_MODEL, ODD_IN)
    d['l1_conv_w'] = nrm((GDN_CONV, 1, GDN_CONV_CH), GDN_CONV ** -0.5)
    d['l1_a_log'] = jnp.log(jax.random.uniform(next(ks), (GDN_HEADS,), f32, 1.0, 16.0))
    dt = jax.random.uniform(next(ks), (GDN_HEADS,), f32, 0.001, 0.1)
    d['l1_dt_bias'] = dt + jnp.log(-jnp.expm1(-dt))
    d['l1_g_o'] = gain(GDN_DV)
    d['l1_w_out'] = lin(GDN_VW, D_MODEL)
    d['l1_w_router'] = lin(D_MODEL, N_EXPERTS)
    d['l1_b_router'] = nrm((N_EXPERTS,), 0.01)
    d['l1_wg'] = nrm((N_EXPERTS, D_MODEL, D_FF_EXPERT), D_MODEL ** -0.5)
    d['l1_wu'] = nrm((N_EXPERTS, D_MODEL, D_FF_EXPERT), D_MODEL ** -0.5)
    d['l1_wd'] = nrm((N_EXPERTS, D_FF_EXPERT, D_MODEL), D_FF_EXPERT ** -0.5)
    d['g_final'] = gain(D_MODEL)
    return d


def reference(x_prompt, x_sample, cache_mla_ckv, cache_mla_krope, cache_fox_k, cache_fox_v, cache_fox_logf,
              state_gdn, state_gdn_conv, page_table, c_prompt, c_sample,
              l0_w_mod, l0_b_mod, l0_g_mix, l0_g_ffn, l0_w_in, l0_g_qa, l0_w_uq, l0_g_kva, l0_w_uk, l0_w_uv,
              l0_b_f, l0_w_out, l0_w1, l0_w3, l0_w2,
              l1_w_mod, l1_b_mod, l1_g_mix, l1_g_ffn, l1_w_in, l1_conv_w, l1_a_log, l1_dt_bias, l1_g_o,
              l1_w_out, l1_w_router, l1_b_router, l1_wg, l1_wu, l1_wd, g_final):
    past = page_table.shape[1] * cache_mla_ckv.shape[1]
    pos_p = jnp.arange(x_prompt.shape[1], dtype=jnp.int32)
    pos_s = past + jnp.arange(x_sample.shape[1], dtype=jnp.int32)
    xp, xs = x_prompt, x_sample
    for layer in range(DEPTH):
        if layer % 2 == 0:
            mp = adaln(c_prompt, l0_w_mod, l0_b_mod)
            ms = adaln(c_sample, l0_w_mod, l0_b_mod)
            a_p, p_att = even_mixer_prompt(modulate(xp, l0_g_mix, mp[0], mp[1]), pos_p,
                                           l0_w_in, l0_g_qa, l0_w_uq, l0_g_kva, l0_w_uk, l0_w_uv, l0_b_f, l0_w_out)
            a_s, s_att = even_mixer_sample(modulate(xs, l0_g_mix, ms[0], ms[1]), pos_s, page_table,
                                           cache_mla_ckv, cache_mla_krope, cache_fox_k, cache_fox_v, cache_fox_logf,
                                           l0_w_in, l0_g_qa, l0_w_uq, l0_g_kva, l0_w_uk, l0_w_uv, l0_b_f, l0_w_out)
            xp = xp + mp[2] * a_p
            xs = xs + ms[2] * a_s
            xp = xp + mp[5] * swiglu(modulate(xp, l0_g_ffn, mp[3], mp[4]), l0_w1, l0_w3, l0_w2)
            xs = xs + ms[5] * swiglu(modulate(xs, l0_g_ffn, ms[3], ms[4]), l0_w1, l0_w3, l0_w2)
        else:
            mp = adaln(c_prompt, l1_w_mod, l1_b_mod)
            ms = adaln(c_sample, l1_w_mod, l1_b_mod)
            conv0 = jnp.zeros((xp.shape[0], GDN_CONV - 1, GDN_CONV_CH), xp.dtype)
            s0 = jnp.zeros((xp.shape[0], GDN_HEADS, GDN_DK, GDN_DV), jnp.float32)
            a_p, p_gdn, p_conv = gdn_mixer(modulate(xp, l1_g_mix, mp[0], mp[1]), conv0, s0,
                                           l1_w_in, l1_conv_w, l1_a_log, l1_dt_bias, l1_g_o, l1_w_out, True)
            a_s, s_gdn, s_conv = gdn_mixer(modulate(xs, l1_g_mix, ms[0], ms[1]), state_gdn_conv, state_gdn,
                                           l1_w_in, l1_conv_w, l1_a_log, l1_dt_bias, l1_g_o, l1_w_out, False)
            xp = xp + mp[2] * a_p
            xs = xs + ms[2] * a_s
            xp = xp + mp[5] * moe_swiglu(modulate(xp, l1_g_ffn, mp[3], mp[4]), l1_w_router, l1_b_router, l1_wg, l1_wu, l1_wd)
            xs = xs + ms[5] * moe_swiglu(modulate(xs, l1_g_ffn, ms[3], ms[4]), l1_w_router, l1_b_router, l1_wg, l1_wu, l1_wd)
    y_prompt = rmsnorm(xp, g_final)
    y_sample = rmsnorm(xs, g_final)
    p_ckv, p_krope, p_fox_k, p_fox_v, p_fox_logf = p_att
    s_ckv, s_krope, s_fox_k, s_fox_v, s_fox_logf = s_att
    return (y_prompt, y_sample, p_ckv, p_krope, p_fox_k, p_fox_v, p_fox_logf, p_gdn, p_conv,
            s_ckv, s_krope, s_fox_k, s_fox_v, s_fox_logf, s_gdn, s_conv)
```

```python
import functools

import jax
import jax.numpy as jnp
import numpy as np
from jax import lax
from jax.experimental import pallas as pl
from jax.experimental.pallas import tpu as pltpu

D_MODEL = 2048
MLA_HEADS = 8
MLA_NOPE = 128
MLA_ROPE = 64
MLA_KV_LORA = 256
FOX_HEADS = 8
FOX_KV_HEADS = 2
FOX_DIM = 128
PAGE_SIZE = 128
MLA_SCALE = (MLA_NOPE + MLA_ROPE) ** -0.5
FOX_SCALE = FOX_DIM ** -0.5

BF16 = jnp.bfloat16
F32 = jnp.float32
VMEM_LIMIT = 56 * 1024 * 1024


DEC_PAGES_PER_STEP = 8


def _decode_kernel(pt_ref, qa_ref, qr_ref, cn_ref, rn_ref, fq_ref, fkn_ref, fvn_ref, lfn_ref, tri_ref, *rest,
                   pages_per_step):
    pc = pages_per_step
    ckv = rest[0 * pc:1 * pc]
    krp = rest[1 * pc:2 * pc]
    fkk = rest[2 * pc:3 * pc]
    fvv = rest[3 * pc:4 * pc]
    lgf = rest[4 * pc:5 * pc]
    olat_ref, ofox_ref = rest[5 * pc:5 * pc + 2]
    m1, l1, a1, m2, l2, a2, csum = rest[5 * pc + 2:]
    j = pl.program_id(1)
    nj = pl.num_programs(1)

    qa = qa_ref[0].astype(BF16)
    qr = qr_ref[0].astype(BF16)
    fq = fq_ref[0].astype(BF16)
    head = lax.broadcasted_iota(jnp.int32, (FOX_HEADS, 2 * PAGE_SIZE), 0)
    col = lax.broadcasted_iota(jnp.int32, (FOX_HEADS, 2 * PAGE_SIZE), 1)
    own = (col % FOX_KV_HEADS) == (head // (FOX_HEADS // FOX_KV_HEADS))

    @pl.when(j == 0)
    def _():
        cn = cn_ref[0]
        rn = rn_ref[0]
        s_new = (jnp.sum(qa.astype(F32) * cn.astype(BF16).astype(F32), axis=-1, keepdims=True)
                 + jnp.sum(qr.astype(F32) * rn.astype(BF16).astype(F32), axis=-1, keepdims=True)) * MLA_SCALE
        m1[...] = s_new
        l1[...] = jnp.ones_like(s_new)
        a1[...] = jnp.broadcast_to(cn.astype(BF16).astype(F32), a1.shape)
        fkn = fkn_ref[0].astype(BF16).astype(F32)
        fvn = fvn_ref[0].astype(BF16).astype(F32)
        g = FOX_HEADS // FOX_KV_HEADS
        kn_h = jnp.concatenate([jnp.broadcast_to(fkn[i:i + 1], (g, FOX_DIM)) for i in range(FOX_KV_HEADS)], axis=0)
        vn_h = jnp.concatenate([jnp.broadcast_to(fvn[i:i + 1], (g, FOX_DIM)) for i in range(FOX_KV_HEADS)], axis=0)
        m2[...] = jnp.sum(fq.astype(F32) * kn_h, axis=-1, keepdims=True) * FOX_SCALE
        l2[...] = jnp.ones((FOX_HEADS, 1), F32)
        a2[...] = vn_h
        csum[...] = jnp.broadcast_to(lfn_ref[0].reshape(FOX_HEADS, 1), csum.shape)

    cbs = [ckv[k][...].astype(BF16) for k in range(pc)]
    s_list = []
    for k in range(pc):
        rbt = krp[k][...].astype(BF16)
        s = lax.dot_general(qa, cbs[k], (((1,), (1,)), ((), ())), preferred_element_type=F32)
        s = s + jnp.dot(qr, rbt, preferred_element_type=F32)
        s_list.append(s * MLA_SCALE)
    m_prev = m1[...]
    m_new = m_prev
    for s in s_list:
        m_new = jnp.maximum(m_new, jnp.max(s, axis=-1, keepdims=True))
    alpha = jnp.exp(m_prev - m_new)
    l_new = l1[...] * alpha
    acc = a1[...] * alpha
    for k in range(pc):
        p = jnp.exp(s_list[k] - m_new)
        l_new = l_new + jnp.sum(p, axis=-1, keepdims=True)
        acc = acc + jnp.dot(p.astype(BF16), cbs[k], preferred_element_type=F32)
    m1[...] = m_new
    l1[...] = l_new
    a1[...] = acc

    tri = tri_ref[...]
    c_run = csum[...]
    s_list = []
    vbs = []
    for k in range(pc):
        lft = lgf[k][...]
        suf = jnp.dot(lft, tri, preferred_element_type=F32, precision=lax.Precision.HIGHEST)
        bias = c_run + suf[:, :2 * PAGE_SIZE]
        c_run = c_run + suf[:, 2 * PAGE_SIZE:]
        kb = fkk[k][...].astype(BF16)
        vbs.append(fvv[k][...].astype(BF16))
        s = lax.dot_general(fq, kb, (((1,), (1,)), ((), ())), preferred_element_type=F32) * FOX_SCALE + bias
        s_list.append(jnp.where(own, s, -jnp.inf))
    csum[...] = c_run
    m_prev = m2[...]
    m_new = m_prev
    for s in s_list:
        m_new = jnp.maximum(m_new, jnp.max(s, axis=-1, keepdims=True))
    alpha = jnp.exp(m_prev - m_new)
    l_new = l2[...] * alpha
    acc = a2[...] * alpha
    for k in range(pc):
        p = jnp.exp(s_list[k] - m_new)
        l_new = l_new + jnp.sum(p, axis=-1, keepdims=True)
        acc = acc + jnp.dot(p.astype(BF16), vbs[k], preferred_element_type=F32)
    m2[...] = m_new
    l2[...] = l_new
    a2[...] = acc

    @pl.when(j == nj - 1)
    def _():
        olat_ref[0] = a1[...] / l1[...]
        ofox_ref[0] = a2[...] / l2[...]


def _suffix_matrix():
    j = np.arange(PAGE_SIZE)[:, None]
    s = np.arange(2 * PAGE_SIZE)[None, :] // FOX_KV_HEADS
    strict = (j > s).astype(np.float32)
    return jnp.asarray(np.concatenate([strict, np.ones((PAGE_SIZE, 2 * PAGE_SIZE), np.float32)], axis=1))


def decode_attention(page_table, q_abs, q_rope, c_new, r_new, fq, fk_new, fv_new, lf_new,
                     cache_ckv, cache_krope, cache_fox_k, cache_fox_v, cache_logf):
    db, n_pages = page_table.shape
    pc = DEC_PAGES_PER_STEP
    n_pool = cache_ckv.shape[0]
    fk2 = cache_fox_k.reshape(n_pool, PAGE_SIZE * FOX_KV_HEADS, FOX_DIM)
    fv2 = cache_fox_v.reshape(n_pool, PAGE_SIZE * FOX_KV_HEADS, FOX_DIM)
    krope_t = jnp.swapaxes(cache_krope, 1, 2)
    logf_t = jnp.swapaxes(cache_logf, 1, 2)

    def per_b(shape):
        return pl.BlockSpec((1,) + shape, lambda b, j, pt: (b,) + (0,) * len(shape))

    def page_spec(shape, k, reverse):
        def imap(b, j, pt):
            idx = j * pc + k
            if reverse:
                idx = n_pages - 1 - idx
            return (pt[b, idx],) + (0,) * len(shape)
        return pl.BlockSpec((None,) + shape, imap)

    in_specs = [per_b((8, 256)), per_b((8, 64)), per_b((1, 256)), per_b((1, 64)), per_b((8, 128)),
                per_b((2, 128)), per_b((2, 128)), per_b((1, 8)),
                pl.BlockSpec((PAGE_SIZE, 4 * PAGE_SIZE), lambda b, j, pt: (0, 0))]
    in_specs += [page_spec((PAGE_SIZE, MLA_KV_LORA), k, True) for k in range(pc)]
    in_specs += [page_spec((MLA_ROPE, PAGE_SIZE), k, True) for k in range(pc)]
    in_specs += [page_spec((2 * PAGE_SIZE, FOX_DIM), k, True) for k in range(pc)]
    in_specs += [page_spec((2 * PAGE_SIZE, FOX_DIM), k, True) for k in range(pc)]
    in_specs += [page_spec((FOX_HEADS, PAGE_SIZE), k, True) for k in range(pc)]
    grid_spec = pltpu.PrefetchScalarGridSpec(
        num_scalar_prefetch=1,
        grid=(db, n_pages // pc),
        in_specs=in_specs,
        out_specs=[per_b((8, 256)), per_b((8, 128))],
        scratch_shapes=[pltpu.VMEM((8, 1), F32), pltpu.VMEM((8, 1), F32), pltpu.VMEM((8, 256), F32),
                        pltpu.VMEM((8, 1), F32), pltpu.VMEM((8, 1), F32), pltpu.VMEM((8, 128), F32),
                        pltpu.VMEM((8, 256), F32)],
    )
    return pl.pallas_call(
        functools.partial(_decode_kernel, pages_per_step=pc),
        grid_spec=grid_spec,
        out_shape=[jax.ShapeDtypeStruct((db, 8, 256), F32), jax.ShapeDtypeStruct((db, 8, 128), F32)],
        compiler_params=pltpu.CompilerParams(dimension_semantics=("arbitrary", "arbitrary"),
                                             vmem_limit_bytes=VMEM_LIMIT),
        name="decode_attention",
    )(page_table, q_abs, q_rope, c_new, r_new, fq, fk_new, fv_new, lf_new, _suffix_matrix(),
      *([cache_ckv] * pc), *([krope_t] * pc), *([fk2] * pc), *([fv2] * pc), *([logf_t] * pc))


def _cparams(*sem):
    return pltpu.CompilerParams(dimension_semantics=sem, vmem_limit_bytes=VMEM_LIMIT)


def _matmul_kernel(x_ref, w_ref, *rest, silu_lhs, has_bias):
    o_ref = rest[-1]
    x = x_ref[...]
    if silu_lhs:
        x = x.astype(F32)
        x = x * jax.nn.sigmoid(x)
    acc = jnp.dot(x.astype(BF16), w_ref[...].astype(BF16), preferred_element_type=F32)
    if has_bias:
        acc = acc + rest[0][...]
    o_ref[...] = acc.astype(o_ref.dtype)


def matmul(x, w, *, tm, tn, name, out_dtype=F32, bias=None, silu_lhs=False):
    m, k = x.shape
    n = w.shape[1]
    tm = min(tm, m)
    tn = min(tn, n)
    in_specs = [pl.BlockSpec((tm, k), lambda i, j: (i, 0)), pl.BlockSpec((k, tn), lambda i, j: (0, j))]
    args = [x, w]
    if bias is not None:
        in_specs.append(pl.BlockSpec((1, tn), lambda i, j: (0, j)))
        args.append(bias.reshape(1, n))
    return pl.pallas_call(
        functools.partial(_matmul_kernel, silu_lhs=silu_lhs, has_bias=bias is not None),
        grid=(pl.cdiv(m, tm), pl.cdiv(n, tn)),
        in_specs=in_specs,
        out_specs=pl.BlockSpec((tm, tn), lambda i, j: (i, j)),
        out_shape=jax.ShapeDtypeStruct((m, n), out_dtype),
        compiler_params=_cparams("parallel", "arbitrary"),
        name=name,
    )(*args)


def _headwise_kernel(x_ref, w_ref, o_ref, *, transpose_w):
    x = x_ref[...].astype(BF16)
    w = w_ref[...].astype(BF16)
    dims = (((1,), (1,)), ((), ())) if transpose_w else (((1,), (0,)), ((), ()))
    o_ref[...] = lax.dot_general(x, w, dims, preferred_element_type=F32)


def headwise_matmul(x, w, *, heads, transpose_w, name):
    m = x.shape[0]
    dx = x.shape[1] // heads
    c = w.shape[0]
    dw = w.shape[1] // heads
    do = c if transpose_w else dw
    return pl.pallas_call(
        functools.partial(_headwise_kernel, transpose_w=transpose_w),
        grid=(heads,),
        in_specs=[pl.BlockSpec((m, dx), lambda h: (0, h)), pl.BlockSpec((c, dw), lambda h: (0, h))],
        out_specs=pl.BlockSpec((m, do), lambda h: (0, h)),
        out_shape=jax.ShapeDtypeStruct((m, heads * do), F32),
        compiler_params=_cparams("arbitrary"),
        name=name,
    )(x, w)


ROW_TILE = 128
EPS = 1e-6


def _table_row_block(i, tiles_per_prompt_batch, n_prompt_batches):
    return jnp.minimum(i // tiles_per_prompt_batch, n_prompt_batches)


def _modulate_kernel(x_ref, g_ref, *rest, modulated):
    o_ref = rest[-1]
    x = x_ref[...]
    y = x * lax.rsqrt(jnp.mean(x * x, axis=-1, keepdims=True) + EPS)
    y = y * g_ref[...]
    if modulated:
        shift_ref, scale_ref = rest[0], rest[1]
        y = y * (1.0 + scale_ref[...]) + shift_ref[...]
    o_ref[...] = y.astype(o_ref.dtype)


def modulate(x, gain, tables, k_shift, *, tiles_per_batch, n_batches, out_dtype, name):
    m, d = x.shape
    in_specs = [pl.BlockSpec((ROW_TILE, d), lambda i: (i, 0)), pl.BlockSpec((1, d), lambda i: (0, 0))]
    args = [x, gain.reshape(1, d)]
    if tables is not None:
        for k in (k_shift, k_shift + 1):
            in_specs.append(pl.BlockSpec(
                (None, ROW_TILE, d), lambda i, k=k: (k, _table_row_block(i, tiles_per_batch, n_batches), 0)))
            args.append(tables)
    return pl.pallas_call(
        functools.partial(_modulate_kernel, modulated=tables is not None),
        grid=(m // ROW_TILE,),
        in_specs=in_specs,
        out_specs=pl.BlockSpec((ROW_TILE, d), lambda i: (i, 0)),
        out_shape=jax.ShapeDtypeStruct((m, d), out_dtype),
        compiler_params=_cparams("parallel"),
        name=name,
    )(*args)


def _residual_kernel(x_ref, gate_ref, a_ref, *rest, two_terms):
    o_ref = rest[-1]
    a = a_ref[...]
    if two_terms:
        a2_ref, w1_ref, w2_ref = rest[0], rest[1], rest[2]
        a = w1_ref[...] * a + w2_ref[...] * a2_ref[...]
    o_ref[...] = x_ref[...] + gate_ref[...] * a


def residual(x, tables, k_gate, a, *, tiles_per_batch, n_batches, name, a2=None, w1=None, w2=None):
    m, d = x.shape
    row = pl.BlockSpec((ROW_TILE, d), lambda i: (i, 0))
    in_specs = [row, pl.BlockSpec((None, ROW_TILE, d),
                                  lambda i: (k_gate, _table_row_block(i, tiles_per_batch, n_batches), 0)), row]
    args = [x, tables, a]
    if a2 is not None:
        col = pl.BlockSpec((ROW_TILE, 1), lambda i: (i, 0))
        in_specs += [row, col, col]
        args += [a2, w1, w2]
    return pl.pallas_call(
        functools.partial(_residual_kernel, two_terms=a2 is not None),
        grid=(m // ROW_TILE,),
        in_specs=in_specs,
        out_specs=row,
        out_shape=jax.ShapeDtypeStruct((m, d), F32),
        compiler_params=_cparams("parallel"),
        name=name,
    )(*args)


def _swiglu_kernel(te_ref, tv_ref, x_ref, wg_ref, wu_ref, wd_ref, o_ref):
    t = pl.program_id(0)
    f = pl.program_id(1)

    @pl.when(f == 0)
    def _():
        o_ref[...] = jnp.zeros_like(o_ref)

    @pl.when(tv_ref[t] > 0)
    def _():
        x = x_ref[...]
        g = jnp.dot(x, wg_ref[...].astype(BF16), preferred_element_type=F32)
        u = jnp.dot(x, wu_ref[...].astype(BF16), preferred_element_type=F32)
        a = (g * jax.nn.sigmoid(g) * u).astype(BF16)
        o_ref[...] += jnp.dot(a, wd_ref[...].astype(BF16), preferred_element_type=F32)


def grouped_swiglu(x, wg, wu, wd, tile_expert, tile_valid, *, tm, tf, name):
    m, d = x.shape
    f_total = wg.shape[2]
    nf = f_total // tf

    def f_eff(t, f, tv):
        return jnp.where(tv[t] > 0, f, nf - 1)

    grid_spec = pltpu.PrefetchScalarGridSpec(
        num_scalar_prefetch=2,
        grid=(m // tm, nf),
        in_specs=[pl.BlockSpec((tm, d), lambda t, f, te, tv: (t, 0)),
                  pl.BlockSpec((None, d, tf), lambda t, f, te, tv: (te[t], 0, f_eff(t, f, tv))),
                  pl.BlockSpec((None, d, tf), lambda t, f, te, tv: (te[t], 0, f_eff(t, f, tv))),
                  pl.BlockSpec((None, tf, d), lambda t, f, te, tv: (te[t], f_eff(t, f, tv), 0))],
        out_specs=pl.BlockSpec((tm, d), lambda t, f, te, tv: (t, 0)),
    )
    return pl.pallas_call(
        _swiglu_kernel,
        grid_spec=grid_spec,
        out_shape=jax.ShapeDtypeStruct((m, d), F32),
        compiler_params=_cparams("arbitrary", "arbitrary"),
        name=name,
    )(tile_expert, tile_valid, x, wg, wu, wd)


def _flash_kernel(q_ref, k_ref, v_ref, *rest, scale, use_bias):
    if use_bias:
        fq_ref, fk_ref = rest[0], rest[1]
        rest = rest[2:]
    o_ref, m_s, l_s, acc_s = rest
    qi = pl.program_id(2)
    ki = pl.program_id(3)

    @pl.when(ki == 0)
    def _():
        m_s[...] = jnp.full_like(m_s, -jnp.inf)
        l_s[...] = jnp.zeros_like(l_s)
        acc_s[...] = jnp.zeros_like(acc_s)

    @pl.when(ki <= qi)
    def _():
        q = q_ref[...]
        k = k_ref[...]
        s = lax.dot_general(q, k, (((1,), (1,)), ((), ())), preferred_element_type=F32) * scale
        if use_bias:
            s = s + fq_ref[...] - fk_ref[...]
        row = lax.broadcasted_iota(jnp.int32, s.shape, 0)
        col = lax.broadcasted_iota(jnp.int32, s.shape, 1)
        s = jnp.where(jnp.logical_and(ki == qi, col > row), -jnp.inf, s)
        m_prev = m_s[...]
        m_new = jnp.maximum(m_prev, jnp.max(s, axis=-1, keepdims=True))
        alpha = jnp.exp(m_prev - m_new)
        p = jnp.exp(s - m_new)
        l_s[...] = l_s[...] * alpha + jnp.sum(p, axis=-1, keepdims=True)
        acc_s[...] = acc_s[...] * alpha + jnp.dot(p.astype(BF16), v_ref[...], preferred_element_type=F32)
        m_s[...] = m_new

    @pl.when(ki == qi)
    def _():
        o_ref[...] = (acc_s[...] / l_s[...]).astype(o_ref.dtype)


def flash_attention(q, k, v, *, scale, tile, name, fq=None, fk=None):
    b, h, t, dk = q.shape
    hk = k.shape[1]
    dv = v.shape[3]
    g = h // hk
    nt = t // tile
    in_specs = [pl.BlockSpec((None, None, tile, dk), lambda b_, h_, i, j: (b_, h_, i, 0)),
                pl.BlockSpec((None, None, tile, dk), lambda b_, h_, i, j: (b_, h_ // g, jnp.minimum(i, j), 0)),
                pl.BlockSpec((None, None, tile, dv), lambda b_, h_, i, j: (b_, h_ // g, jnp.minimum(i, j), 0))]
    args = [q, k, v]
    if fq is not None:
        in_specs += [pl.BlockSpec((None, None, tile, 1), lambda b_, h_, i, j: (b_, h_, i, 0)),
                     pl.BlockSpec((None, None, 1, tile), lambda b_, h_, i, j: (b_, h_, 0, jnp.minimum(i, j)))]
        args += [fq, fk]
    return pl.pallas_call(
        functools.partial(_flash_kernel, scale=scale, use_bias=fq is not None),
        grid=(b, h, nt, nt),
        in_specs=in_specs,
        out_specs=pl.BlockSpec((None, tile, dv), lambda b_, h_, i, j: (b_, i, h_)),
        out_shape=jax.ShapeDtypeStruct((b, t, h * dv), F32),
        scratch_shapes=[pltpu.VMEM((tile, 1), F32), pltpu.VMEM((tile, 1), F32), pltpu.VMEM((tile, dv), F32)],
        compiler_params=_cparams("parallel", "parallel", "arbitrary", "arbitrary"),
        name=name,
    )(*args)


GDN_HEADS = 16
GDN_D = 128
GDN_CHUNK = 64
HIGHEST = lax.Precision.HIGHEST


def _gdn_chunk_kernel(q_ref, k_ref, v_ref, g_ref, gt_ref, beta_ref, lincl_ref, o_ref, sfin_ref, state):
    n = pl.program_id(1)
    c = GDN_CHUNK

    @pl.when(n == 0)
    def _():
        state[...] = jnp.zeros_like(state)

    lincl = lincl_ref[...]
    gc_cols = jnp.dot(lincl, g_ref[...], preferred_element_type=F32, precision=HIGHEST)
    gc_rows = lax.dot_general(gt_ref[...], lincl, (((1,), (1,)), ((), ())),
                              preferred_element_type=F32, precision=HIGHEST)
    beta = beta_ref[...]
    row = lax.broadcasted_iota(jnp.int32, (c, c), 0)
    col = lax.broadcasted_iota(jnp.int32, (c, c), 1)
    incl = row >= col
    strict = row > col
    eye = (row == col).astype(F32)

    qs, ks, vs, decays, a_mats, gccs, bcols = [], [], [], [], [], [], []
    for h in range(GDN_HEADS):
        sl = slice(h * GDN_D, (h + 1) * GDN_D)
        qh, kh, vh = q_ref[:, sl], k_ref[:, sl], v_ref[:, sl]
        gcc = gc_cols[:, h:h + 1]
        gcr = gc_rows[h:h + 1, :]
        bcol = beta[:, h:h + 1]
        decay = jnp.exp(jnp.where(incl, gcc - gcr, -jnp.inf))
        kb = kh.astype(BF16)
        kk = lax.dot_general(kb, kb, (((1,), (1,)), ((), ())), preferred_element_type=F32)
        a_mats.append(jnp.where(strict, decay, 0.0) * bcol * kk)
        qs.append(qh), ks.append(kh), vs.append(vh), decays.append(decay), gccs.append(gcc), bcols.append(bcol)

    nmat = -jnp.stack(a_mats)
    pmat = eye[None] + nmat
    for _ in range(5):
        nmat = jnp.einsum('hij,hjk->hik', nmat, nmat, preferred_element_type=F32, precision=HIGHEST)
        pmat = pmat + jnp.einsum('hij,hjk->hik', pmat, nmat, preferred_element_type=F32, precision=HIGHEST)

    for h in range(GDN_HEADS):
        sl = slice(h * GDN_D, (h + 1) * GDN_D)
        qh, kh, vh, decay, gcc, bcol = qs[h], ks[h], vs[h], decays[h], gccs[h], bcols[h]
        egc = jnp.exp(gcc)
        tinv = pmat[h]
        w = jnp.dot(tinv, (bcol * egc) * kh, preferred_element_type=F32, precision=HIGHEST)
        u = jnp.dot(tinv, bcol * vh, preferred_element_type=F32, precision=HIGHEST)
        qb = qh.astype(BF16)
        kb = kh.astype(BF16)
        qk = lax.dot_general(qb, kb, (((1,), (1,)), ((), ())), preferred_element_type=F32) * decay
        s_old = state[h]
        sb = s_old.astype(BF16)
        u_corr = u - jnp.dot(w.astype(BF16), sb, preferred_element_type=F32)
        ub = u_corr.astype(BF16)
        o = egc * jnp.dot(qb, sb, preferred_element_type=F32) + jnp.dot(qk.astype(BF16), ub, preferred_element_type=F32)
        glast = gcc[c - 1:c, :]
        k_end = (kh * jnp.exp(glast - gcc)).astype(BF16)
        s_new = jnp.exp(glast) * s_old + lax.dot_general(k_end, ub, (((0,), (0,)), ((), ())),
                                                         preferred_element_type=F32)
        state[h] = s_new
        o_ref[:, sl] = o

    @pl.when(n == pl.num_programs(1) - 1)
    def _():
        sfin_ref[...] = state[...]


def gdn_chunked(q, k, v, g, beta):
    b, t, hd = q.shape
    c = GDN_CHUNK
    gt = jnp.swapaxes(g.reshape(b, t // c, c, GDN_HEADS), 2, 3)
    lincl = jnp.asarray(np.tril(np.ones((c, c), np.float32)))
    seq = pl.BlockSpec((None, c, hd), lambda b_, n: (b_, n, 0))
    gate = pl.BlockSpec((None, c, GDN_HEADS), lambda b_, n: (b_, n, 0))
    return pl.pallas_call(
        _gdn_chunk_kernel,
        grid=(b, t // c),
        in_specs=[seq, seq, seq, gate, pl.BlockSpec((None, None, GDN_HEADS, c), lambda b_, n: (b_, n, 0, 0)), gate,
                  pl.BlockSpec((c, c), lambda b_, n: (0, 0))],
        out_specs=[seq, pl.BlockSpec((None, GDN_HEADS, GDN_D, GDN_D), lambda b_, n: (b_, 0, 0, 0))],
        out_shape=[jax.ShapeDtypeStruct((b, t, hd), F32),
                   jax.ShapeDtypeStruct((b, GDN_HEADS, GDN_D, GDN_D), F32)],
        scratch_shapes=[pltpu.VMEM((GDN_HEADS, GDN_D, GDN_D), F32)],
        compiler_params=_cparams("parallel", "arbitrary"),
        name="gdn_chunked",
    )(q, k, v, g, gt, beta, lincl)


def _gdn_step_kernel(q_ref, k_ref, kcol_ref, v_ref, g_ref, beta_ref, s_ref, o_ref, snew_ref):
    eg = jnp.exp(g_ref[...])
    for h in range(GDN_HEADS):
        sl = slice(h * GDN_D, (h + 1) * GDN_D)
        s1 = s_ref[h] * eg[:, h:h + 1]
        k8 = jnp.broadcast_to(k_ref[:, sl], (8, GDN_D)).astype(BF16)
        ks = jnp.dot(k8, s1.astype(BF16), preferred_element_type=F32)[0:1]
        u = beta_ref[:, h:h + 1] * (v_ref[:, sl] - ks)
        s2 = s1 + kcol_ref[h] * u
        q8 = jnp.broadcast_to(q_ref[:, sl], (8, GDN_D)).astype(BF16)
        o_ref[:, sl] = jnp.dot(q8, s2.astype(BF16), preferred_element_type=F32)[0:1]
        snew_ref[h] = s2


def gdn_step(q, k, v, g, beta, s0):
    b = q.shape[0]
    hd = GDN_HEADS * GDN_D
    kcol = k.reshape(b, GDN_HEADS, GDN_D, 1)
    vec = pl.BlockSpec((None, 1, hd), lambda i: (i, 0, 0))
    gate = pl.BlockSpec((None, 1, GDN_HEADS), lambda i: (i, 0, 0))
    st = pl.BlockSpec((None, GDN_HEADS, GDN_D, GDN_D), lambda i: (i, 0, 0, 0))
    return pl.pallas_call(
        _gdn_step_kernel,
        grid=(b,),
        in_specs=[vec, vec, pl.BlockSpec((None, GDN_HEADS, GDN_D, 1), lambda i: (i, 0, 0, 0)), vec, gate, gate, st],
        out_specs=[vec, st],
        out_shape=[jax.ShapeDtypeStruct((b, 1, hd), F32), jax.ShapeDtypeStruct(s0.shape, F32)],
        compiler_params=_cparams("parallel"),
        name="gdn_step",
    )(q, k, kcol, v, g, beta, s0)


ROPE_THETA = 10000.0
MLA_Q_LORA = 512
N_MOD = 6
TOP_K = 2
GDN_CONV = 4
MATMUL_ROWS = 1040
MATMUL_COLS = 512
MOE_ROWS = 1024
FF_COLS = 256
FLASH_TILE = 1024


def _pick_tile(m, target):
    best = None
    for t in range(16, min(m, target) + 1, 16):
        if m % t == 0:
            best = t
    return best if best is not None else m


def _rmsnorm(x, gain):
    y = x * lax.rsqrt(jnp.mean(x * x, axis=-1, keepdims=True) + EPS)
    return y * gain


def _rope(x, cos, sin):
    half = x.shape[-1] // 2
    x1, x2 = x[..., :half], x[..., half:]
    c, s = cos[:, None, :], sin[:, None, :]
    return jnp.concatenate([x1 * c - x2 * s, x1 * s + x2 * c], axis=-1)


def _route(top_idx, n_experts, tm):
    m = top_idx.shape[0]
    flat_e = top_idx.reshape(-1)
    onehot = (flat_e[:, None] == jnp.arange(n_experts, dtype=jnp.int32)[None, :]).astype(jnp.int32)
    rank = jnp.sum((jnp.cumsum(onehot, axis=0) - onehot) * onehot, axis=1)
    counts = jnp.sum(onehot, axis=0)
    tiles_e = (counts + tm - 1) // tm
    tile_end = jnp.cumsum(tiles_e)
    tile_start = tile_end - tiles_e
    pos = tile_start[flat_e] * tm + rank
    n_tiles = (TOP_K * m + n_experts * (tm - 1)) // tm
    src = jnp.zeros((n_tiles * tm,), jnp.int32).at[pos].set(jnp.arange(TOP_K * m, dtype=jnp.int32) // TOP_K)
    t = jnp.arange(n_tiles, dtype=jnp.int32)
    tile_valid = (t < tile_end[-1]).astype(jnp.int32)
    t_clamped = jnp.minimum(t, tile_end[-1] - 1)
    tile_expert = jnp.sum((t_clamped[:, None] >= tile_end[None, :]).astype(jnp.int32), axis=1)
    return src, pos.reshape(m, TOP_K), tile_expert.astype(jnp.int32), tile_valid


def kernel(x_prompt, x_sample, cache_mla_ckv, cache_mla_krope, cache_fox_k, cache_fox_v, cache_fox_logf, state_gdn, state_gdn_conv, page_table, c_prompt, c_sample, l0_w_mod, l0_b_mod, l0_g_mix, l0_g_ffn, l0_w_in, l0_g_qa, l0_w_uq, l0_g_kva, l0_w_uk, l0_w_uv, l0_b_f, l0_w_out, l0_w1, l0_w3, l0_w2, l1_w_mod, l1_b_mod, l1_g_mix, l1_g_ffn, l1_w_in, l1_conv_w, l1_a_log, l1_dt_bias, l1_g_o, l1_w_out, l1_w_router, l1_b_router, l1_wg, l1_wu, l1_wd, g_final):
    nb, t, d = x_prompt.shape
    db = x_sample.shape[0]
    assert x_sample.shape[1] == 1 and db == ROW_TILE and t % ROW_TILE == 0
    mp = nb * t
    m = mp + db
    past = page_table.shape[1] * cache_mla_ckv.shape[1]
    tpb = t // ROW_TILE
    tm = _pick_tile(m, MATMUL_ROWS)
    tmp = _pick_tile(mp, MATMUL_ROWS)
    mm = functools.partial(matmul, tn=MATMUL_COLS)
    row_kw = dict(tiles_per_batch=tpb, n_batches=nb)

    x = jnp.concatenate([x_prompt.reshape(mp, d), x_sample.reshape(db, d)], axis=0)
    c_all = jnp.concatenate([c_prompt, c_sample], axis=0)

    def mod_tables(w_mod, b_mod, name):
        mod = matmul(c_all, w_mod, tm=nb + db, tn=1024, bias=b_mod, silu_lhs=True, name=name)
        mod = jnp.swapaxes(mod.reshape(nb + db, N_MOD, d), 0, 1)
        return jnp.concatenate([jnp.repeat(mod[:, :nb], ROW_TILE, axis=1), mod[:, nb:]], axis=1)

    pos = jnp.concatenate([jnp.tile(jnp.arange(t, dtype=jnp.int32), nb), jnp.full((db,), past, jnp.int32)])
    half = MLA_ROPE // 2
    inv = ROPE_THETA ** (-jnp.arange(half, dtype=F32) / half)
    ang = pos.astype(F32)[:, None] * inv[None, :]
    cos, sin = jnp.cos(ang), jnp.sin(ang)

    t0 = mod_tables(l0_w_mod, l0_b_mod, "adaln0")
    h = modulate(x, l0_g_mix, t0, 0, out_dtype=BF16, name="mod_mix0", **row_kw)
    p0 = mm(h, l0_w_in, tm=tm, name="proj0")
    nq, nc, nr = MLA_Q_LORA, MLA_KV_LORA, MLA_ROPE
    nfq, nfk = FOX_HEADS * FOX_DIM, FOX_KV_HEADS * FOX_DIM
    offs = np.cumsum([0, nq, nc, nr, nfq, nfk, nfk, FOX_HEADS])
    q_lat, kv_lat, k_r, fq, fk, fv, f_logit = (p0[:, offs[i]:offs[i + 1]] for i in range(7))
    w_uq = l0_w_uq.reshape(nq, MLA_HEADS, MLA_NOPE + MLA_ROPE)
    w_q = jnp.concatenate([w_uq[:, :, :MLA_NOPE].reshape(nq, -1), w_uq[:, :, MLA_NOPE:].reshape(nq, -1)], axis=1)
    qq = mm(_rmsnorm(q_lat, l0_g_qa).astype(BF16), w_q, tm=tm, name="q_up")
    q_nope = qq[:, :MLA_HEADS * MLA_NOPE]
    q_rope = _rope(qq[:, MLA_HEADS * MLA_NOPE:].reshape(m, MLA_HEADS, MLA_ROPE), cos, sin)
    c_kv = _rmsnorm(kv_lat, l0_g_kva)
    k_rope = _rope(k_r[:, None, :], cos, sin)[:, 0]
    lf = jax.nn.log_sigmoid(f_logit + l0_b_f)

    w_uk2 = l0_w_uk.reshape(nc, MLA_HEADS * MLA_NOPE)
    w_uv2 = l0_w_uv.reshape(nc, MLA_HEADS * MLA_NOPE)
    kv = mm(c_kv[:mp].astype(BF16), jnp.concatenate([w_uk2, w_uv2], axis=1), tm=tmp, name="kv_up")

    def heads_first(a, n_heads):
        return jnp.swapaxes(a.reshape(nb, t, n_heads, -1), 1, 2).astype(BF16)

    q_cat = jnp.concatenate([q_nope[:mp].reshape(nb, t, MLA_HEADS, MLA_NOPE),
                             q_rope[:mp].reshape(nb, t, MLA_HEADS, MLA_ROPE)], axis=-1)
    k_cat = jnp.concatenate([kv[:, :MLA_HEADS * MLA_NOPE].reshape(nb, t, MLA_HEADS, MLA_NOPE),
                             jnp.broadcast_to(k_rope[:mp].reshape(nb, t, 1, MLA_ROPE), (nb, t, MLA_HEADS, MLA_ROPE))],
                            axis=-1)
    ftile = _pick_tile(t, FLASH_TILE)
    o_mla_p = flash_attention(heads_first(q_cat, MLA_HEADS), heads_first(k_cat, MLA_HEADS),
                              heads_first(kv[:, MLA_HEADS * MLA_NOPE:], MLA_HEADS),
                              scale=MLA_SCALE, tile=ftile, name="flash_mla")
    fcum = jnp.swapaxes(jnp.cumsum(lf[:mp].reshape(nb, t, FOX_HEADS), axis=1), 1, 2)
    o_fox_p = flash_attention(heads_first(fq[:mp], FOX_HEADS), heads_first(fk[:mp], FOX_KV_HEADS),
                              heads_first(fv[:mp], FOX_KV_HEADS), scale=FOX_SCALE, tile=ftile, name="flash_fox",
                              fq=fcum[..., None], fk=fcum[:, :, None, :])

    q_abs = headwise_matmul(q_nope[mp:], w_uk2, heads=MLA_HEADS, transpose_w=True, name="q_absorb")
    o_lat, o_fox_s = decode_attention(
        page_table, q_abs.reshape(db, MLA_HEADS, nc), q_rope[mp:], c_kv[mp:, None, :], k_rope[mp:, None, :],
        fq[mp:].reshape(db, FOX_HEADS, FOX_DIM), fk[mp:].reshape(db, FOX_KV_HEADS, FOX_DIM),
        fv[mp:].reshape(db, FOX_KV_HEADS, FOX_DIM), lf[mp:, None, :],
        cache_mla_ckv, cache_mla_krope, cache_fox_k, cache_fox_v, cache_fox_logf)
    o_mla_s = headwise_matmul(o_lat.reshape(db, MLA_HEADS * nc), w_uv2, heads=MLA_HEADS, transpose_w=False,
                              name="o_unabsorb")

    o = jnp.concatenate([jnp.concatenate([o_mla_p, o_fox_p], axis=-1).reshape(mp, -1),
                         jnp.concatenate([o_mla_s, o_fox_s.reshape(db, -1)], axis=-1)], axis=0).astype(BF16)
    x = residual(x, t0, 2, mm(o, l0_w_out, tm=tm, name="out0"), name="res_mix0", **row_kw)
    h = modulate(x, l0_g_ffn, t0, 3, out_dtype=BF16, name="mod_ffn0", **row_kw)
    n_row_tiles = m // tm
    ffn = grouped_swiglu(h, l0_w1[None], l0_w3[None], l0_w2[None], jnp.zeros((n_row_tiles,), jnp.int32),
                         jnp.ones((n_row_tiles,), jnp.int32), tm=tm, tf=FF_COLS, name="ffn0")
    x = residual(x, t0, 5, ffn, name="res_ffn0", **row_kw)

    t1 = mod_tables(l1_w_mod, l1_b_mod, "adaln1")
    h = modulate(x, l1_g_mix, t1, 0, out_dtype=BF16, name="mod_mix1", **row_kw)
    p1 = mm(h, l1_w_in, tm=tm, name="proj1")
    hd = GDN_HEADS * GDN_D
    cch = 3 * hd
    qkv, z = p1[:, :cch], p1[:, cch:cch + hd]
    a_gate, b_gate = p1[:, cch + hd:cch + hd + GDN_HEADS], p1[:, cch + hd + GDN_HEADS:cch + hd + 2 * GDN_HEADS]
    cw = l1_conv_w.reshape(GDN_CONV, cch)
    qkv_p = qkv[:mp].reshape(nb, t, cch)
    xin_p = jnp.pad(qkv_p, ((0, 0), (GDN_CONV - 1, 0), (0, 0)))
    y_p = sum(xin_p[:, w:w + t] * cw[w] for w in range(GDN_CONV)).reshape(mp, cch)
    xin_s = jnp.concatenate([state_gdn_conv, qkv[mp:, None, :]], axis=1)
    y_s = sum(xin_s[:, w] * cw[w] for w in range(GDN_CONV))
    y = jax.nn.silu(jnp.concatenate([y_p, y_s], axis=0))

    def l2n(a):
        a = a.reshape(m, GDN_HEADS, GDN_D)
        return (a * lax.rsqrt(jnp.sum(a * a, axis=-1, keepdims=True) + EPS)).reshape(m, hd)

    gq = l2n(y[:, :hd]) * (GDN_D ** -0.5)
    gk = l2n(y[:, hd:2 * hd])
    gv = y[:, 2 * hd:]
    g = -jnp.exp(l1_a_log) * jax.nn.softplus(a_gate + l1_dt_bias)
    beta = jax.nn.sigmoid(b_gate)

    def pr(a):
        return a[:mp].reshape(nb, t, -1)

    def sm(a):
        return a[mp:].reshape(db, 1, -1)

    o_p, p_gdn = gdn_chunked(pr(gq), pr(gk), pr(gv), pr(g), pr(beta))
    o_s, s_gdn = gdn_step(sm(gq), sm(gk), sm(gv), sm(g), sm(beta), state_gdn)
    og = jnp.concatenate([o_p.reshape(mp, hd), o_s.reshape(db, hd)], axis=0).reshape(m, GDN_HEADS, GDN_D)
    og = _rmsnorm(og, l1_g_o) * jax.nn.silu(z.reshape(m, GDN_HEADS, GDN_D))
    x = residual(x, t1, 2, mm(og.reshape(m, hd).astype(BF16), l1_w_out, tm=tm, name="out1"), name="res_mix1",
                 **row_kw)
    h = modulate(x, l1_g_ffn, t1, 3, out_dtype=BF16, name="mod_ffn1", **row_kw)
    n_exp = l1_w_router.shape[1]
    logits = matmul(h, l1_w_router, tm=tm, tn=n_exp, bias=l1_b_router, name="router")
    top_val, top_idx = lax.top_k(logits, TOP_K)
    top_w = jax.nn.softmax(top_val, axis=-1)
    src, slot, tile_expert, tile_valid = _route(top_idx.astype(jnp.int32), n_exp, MOE_ROWS)
    ys = grouped_swiglu(jnp.take(h, src, axis=0), l1_wg, l1_wu, l1_wd, tile_expert, tile_valid,
                        tm=MOE_ROWS, tf=FF_COLS, name="moe")
    x = residual(x, t1, 5, jnp.take(ys, slot[:, 0], axis=0), name="res_moe",
                 a2=jnp.take(ys, slot[:, 1], axis=0), w1=top_w[:, 0:1], w2=top_w[:, 1:2], **row_kw)
    y_out = modulate(x, g_final, None, 0, out_dtype=F32, name="final_norm", **row_kw)

    def split(a, *tail):
        return a[:mp].reshape((nb, t) + tail), a[mp:].reshape((db, 1) + tail)

    y_prompt, y_sample = split(y_out, d)
    p_ckv, s_ckv = split(c_kv, nc)
    p_krope, s_krope = split(k_rope, nr)
    p_fk, s_fk = split(fk, FOX_KV_HEADS, FOX_DIM)
    p_fv, s_fv = split(fv, FOX_KV_HEADS, FOX_DIM)
    p_lf, s_lf = split(lf, FOX_HEADS)
    p_conv = qkv_p[:, t - (GDN_CONV - 1):]
    s_conv = xin_s[:, 1:]
    return (y_prompt, y_sample, p_ckv, p_krope, p_fk, p_fv, p_lf, p_gdn, p_conv,
            s_ckv, s_krope, s_fk, s_fv, s_lf, s_gdn, s_conv)
```

```python
import functools

import jax
import jax.numpy as jnp
import numpy as np
from jax import lax
from jax.experimental import pallas as pl
from jax.experimental.pallas import tpu as pltpu

D_MODEL = 2048
MLA_HEADS = 8
MLA_NOPE = 128
MLA_ROPE = 64
MLA_KV_LORA = 256
FOX_HEADS = 8
FOX_KV_HEADS = 2
FOX_DIM = 128
PAGE_SIZE = 128
MLA_SCALE = (MLA_NOPE + MLA_ROPE) ** -0.5
FOX_SCALE = FOX_DIM ** -0.5

BF16 = jnp.bfloat16
F32 = jnp.float32
VMEM_LIMIT = 56 * 1024 * 1024


def _cparams(*sem):
    return pltpu.CompilerParams(dimension_semantics=sem, vmem_limit_bytes=VMEM_LIMIT)


def _split3(x):
    hi = x.astype(BF16)
    r = x - hi.astype(F32)
    mid = r.astype(BF16)
    lo = (r - mid.astype(F32)).astype(BF16)
    return hi, mid, lo


DEC_PAGES_PER_STEP = 8
DEC_BATCH_PER_STEP = 2
N_CACHES = 5


def _decode_kernel(pt_ref, qa_ref, qr_ref, cn_ref, rn_ref, fq_ref, fkn_ref, fvn_ref, lfn_ref, tri_ref,
                   ckv_hbm, krt_hbm, fk_hbm, fv_hbm, lft_hbm, olat_ref, ofox_ref,
                   bc, br, bk, bv, bl, sems, m1, l1, a1, m2, l2, a2, csum, *, n_pages):
    pc, nbb = DEC_PAGES_PER_STEP, DEC_BATCH_PER_STEP
    nc = n_pages // pc
    total = (qa_ref.shape[0] // nbb) * nc
    caches = ((ckv_hbm, bc), (krt_hbm, br), (fk_hbm, bk), (fv_hbm, bv), (lft_hbm, bl))

    def page_copies(step, slot, for_wait):
        first_b = (step // nc) * nbb
        j = step % nc
        out = []
        for bb in range(nbb):
            for k in range(pc):
                page = 0 if for_wait else pt_ref[first_b + bb, n_pages - 1 - (j * pc + k)]
                for i, (src, dst) in enumerate(caches):
                    out.append(pltpu.make_async_copy(src.at[page], dst.at[slot, bb, k], sems.at[slot, i]))
        return out

    head = lax.broadcasted_iota(jnp.int32, (FOX_HEADS, 2 * PAGE_SIZE), 0)
    col = lax.broadcasted_iota(jnp.int32, (FOX_HEADS, 2 * PAGE_SIZE), 1)
    own = (col % FOX_KV_HEADS) == (head // (FOX_HEADS // FOX_KV_HEADS))
    g = FOX_HEADS // FOX_KV_HEADS

    def step_fn(step, carry):
        slot = step % 2
        first_b = (step // nc) * nbb
        j = step % nc

        @pl.when(step + 1 < total)
        def _():
            for c in page_copies(step + 1, 1 - slot, False):
                c.start()

        for c in page_copies(step, slot, True):
            c.wait()

        @pl.when(j == 0)
        def _():
            for bb in range(nbb):
                b = first_b + bb
                qa = qa_ref[b].astype(BF16).astype(F32)
                qr = qr_ref[b].astype(BF16).astype(F32)
                cn = cn_ref[b].astype(BF16).astype(F32)
                rn = rn_ref[b].astype(BF16).astype(F32)
                m1[bb] = (jnp.sum(qa * cn, axis=-1, keepdims=True)
                          + jnp.sum(qr * rn, axis=-1, keepdims=True)) * MLA_SCALE
                l1[bb] = jnp.ones((MLA_HEADS, 1), F32)
                a1[bb] = jnp.broadcast_to(cn, (MLA_HEADS, MLA_KV_LORA))
                fq = fq_ref[b].astype(BF16).astype(F32)
                fkn = fkn_ref[b].astype(BF16).astype(F32)
                fvn = fvn_ref[b].astype(BF16).astype(F32)
                kn_h = jnp.concatenate([jnp.broadcast_to(fkn[i:i + 1], (g, FOX_DIM)) for i in range(FOX_KV_HEADS)], 0)
                vn_h = jnp.concatenate([jnp.broadcast_to(fvn[i:i + 1], (g, FOX_DIM)) for i in range(FOX_KV_HEADS)], 0)
                m2[bb] = jnp.sum(fq * kn_h, axis=-1, keepdims=True) * FOX_SCALE
                l2[bb] = jnp.ones((FOX_HEADS, 1), F32)
                a2[bb] = vn_h
                csum[bb] = jnp.broadcast_to(lfn_ref[b], (FOX_HEADS, 2 * PAGE_SIZE))

        lf_rows = jnp.concatenate([bl[slot, bb, k] for bb in range(nbb) for k in range(pc)], axis=0)
        n_rows = lf_rows.shape[0]
        suf3 = jnp.dot(jnp.concatenate(_split3(lf_rows), axis=0), tri_ref[...], preferred_element_type=F32)
        suf = suf3[:n_rows] + suf3[n_rows:2 * n_rows] + suf3[2 * n_rows:]

        for bb in range(nbb):
            b = first_b + bb
            qa = qa_ref[b].astype(BF16)
            qr = qr_ref[b].astype(BF16)
            cbs = [bc[slot, bb, k].astype(BF16) for k in range(pc)]
            s_list = []
            for k in range(pc):
                rbt = br[slot, bb, k].astype(BF16)
                s = lax.dot_general(qa, cbs[k], (((1,), (1,)), ((), ())), preferred_element_type=F32)
                s = s + jnp.dot(qr, rbt, preferred_element_type=F32)
                s_list.append(s * MLA_SCALE)
            m_prev = m1[bb]
            m_new = m_prev
            for s in s_list:
                m_new = jnp.maximum(m_new, jnp.max(s, axis=-1, keepdims=True))
            alpha = jnp.exp(m_prev - m_new)
            l_new = l1[bb] * alpha
            acc = a1[bb] * alpha
            for k in range(pc):
                p = jnp.exp(s_list[k] - m_new)
                l_new = l_new + jnp.sum(p, axis=-1, keepdims=True)
                acc = acc + jnp.dot(p.astype(BF16), cbs[k], preferred_element_type=F32)
            m1[bb] = m_new
            l1[bb] = l_new
            a1[bb] = acc

            fq = fq_ref[b].astype(BF16)
            c_run = csum[bb]
            s_list = []
            vbs = []
            for k in range(pc):
                r0 = (bb * pc + k) * FOX_HEADS
                bias = c_run + suf[r0:r0 + FOX_HEADS, :2 * PAGE_SIZE]
                c_run = c_run + suf[r0:r0 + FOX_HEADS, 2 * PAGE_SIZE:]
                kb = bk[slot, bb, k].astype(BF16)
                vbs.append(bv[slot, bb, k].astype(BF16))
                s = lax.dot_general(fq, kb, (((1,), (1,)), ((), ())), preferred_element_type=F32) * FOX_SCALE + bias
                s_list.append(jnp.where(own, s, -jnp.inf))
            csum[bb] = c_run
            m_prev = m2[bb]
            m_new = m_prev
            for s in s_list:
                m_new = jnp.maximum(m_new, jnp.max(s, axis=-1, keepdims=True))
            alpha = jnp.exp(m_prev - m_new)
            l_new = l2[bb] * alpha
            acc = a2[bb] * alpha
            for k in range(pc):
                p = jnp.exp(s_list[k] - m_new)
                l_new = l_new + jnp.sum(p, axis=-1, keepdims=True)
                acc = acc + jnp.dot(p.astype(BF16), vbs[k], preferred_element_type=F32)
            m2[bb] = m_new
            l2[bb] = l_new
            a2[bb] = acc

        @pl.when(j == nc - 1)
        def _():
            for bb in range(nbb):
                olat_ref[first_b + bb] = a1[bb] / l1[bb]
                ofox_ref[first_b + bb] = a2[bb] / l2[bb]

        return carry

    for c in page_copies(0, 0, False):
        c.start()
    lax.fori_loop(0, total, step_fn, 0)


def _suffix_matrix():
    j = np.arange(PAGE_SIZE)[:, None]
    s = np.arange(2 * PAGE_SIZE)[None, :] // FOX_KV_HEADS
    strict = (j > s).astype(np.float32)
    return jnp.asarray(np.concatenate([strict, np.ones((PAGE_SIZE, 2 * PAGE_SIZE), np.float32)], axis=1), BF16)


def decode_attention(page_table, q_abs, q_rope, c_new, r_new, fq, fk_new, fv_new, lf_new,
                     cache_ckv, cache_krope, cache_fox_k, cache_fox_v, cache_logf):
    db, n_pages = page_table.shape
    pc, nbb = DEC_PAGES_PER_STEP, DEC_BATCH_PER_STEP
    assert db % nbb == 0 and n_pages % pc == 0
    n_pool = cache_ckv.shape[0]
    fk2 = cache_fox_k.reshape(n_pool, PAGE_SIZE * FOX_KV_HEADS, FOX_DIM)
    fv2 = cache_fox_v.reshape(n_pool, PAGE_SIZE * FOX_KV_HEADS, FOX_DIM)
    krope_t = jnp.swapaxes(cache_krope, 1, 2)
    logf_t = jnp.swapaxes(cache_logf, 1, 2)
    vmem = pl.BlockSpec(memory_space=pltpu.VMEM)
    hbm = pl.BlockSpec(memory_space=pl.ANY)
    slots = (2, nbb, pc)
    return pl.pallas_call(
        functools.partial(_decode_kernel, n_pages=n_pages),
        in_specs=[pl.BlockSpec(memory_space=pltpu.SMEM)] + [vmem] * 9 + [hbm] * N_CACHES,
        out_specs=[vmem, vmem],
        out_shape=[jax.ShapeDtypeStruct((db, MLA_HEADS, MLA_KV_LORA), F32),
                   jax.ShapeDtypeStruct((db, FOX_HEADS, FOX_DIM), F32)],
        scratch_shapes=[pltpu.VMEM(slots + (PAGE_SIZE, MLA_KV_LORA), F32),
                        pltpu.VMEM(slots + (MLA_ROPE, PAGE_SIZE), F32),
                        pltpu.VMEM(slots + (2 * PAGE_SIZE, FOX_DIM), F32),
                        pltpu.VMEM(slots + (2 * PAGE_SIZE, FOX_DIM), F32),
                        pltpu.VMEM(slots + (FOX_HEADS, PAGE_SIZE), F32),
                        pltpu.SemaphoreType.DMA((2, N_CACHES)),
                        pltpu.VMEM((nbb, MLA_HEADS, 1), F32), pltpu.VMEM((nbb, MLA_HEADS, 1), F32),
                        pltpu.VMEM((nbb, MLA_HEADS, MLA_KV_LORA), F32),
                        pltpu.VMEM((nbb, FOX_HEADS, 1), F32), pltpu.VMEM((nbb, FOX_HEADS, 1), F32),
                        pltpu.VMEM((nbb, FOX_HEADS, FOX_DIM), F32),
                        pltpu.VMEM((nbb, FOX_HEADS, 2 * PAGE_SIZE), F32)],
        compiler_params=pltpu.CompilerParams(vmem_limit_bytes=VMEM_LIMIT),
        name="decode_attention",
    )(page_table, q_abs, q_rope, c_new, r_new, fq, fk_new, fv_new, lf_new, _suffix_matrix(),
      cache_ckv, krope_t, fk2, fv2, logf_t)


def _matmul_kernel(x_ref, w_ref, *rest, silu_lhs, has_bias):
    o_ref = rest[-1]
    x = x_ref[...]
    if silu_lhs:
        x = x.astype(F32)
        x = x * jax.nn.sigmoid(x)
    acc = jnp.dot(x.astype(BF16), w_ref[...].astype(BF16), preferred_element_type=F32)
    if has_bias:
        acc = acc + rest[0][...]
    o_ref[...] = acc.astype(o_ref.dtype)


def matmul(x, w, *, tm, tn, name, out_dtype=F32, bias=None, silu_lhs=False):
    m, k = x.shape
    n = w.shape[1]
    tm = min(tm, m)
    tn = min(tn, n)
    in_specs = [pl.BlockSpec((tm, k), lambda i, j: (i, 0)), pl.BlockSpec((k, tn), lambda i, j: (0, j))]
    args = [x, w]
    if bias is not None:
        in_specs.append(pl.BlockSpec((1, tn), lambda i, j: (0, j)))
        args.append(bias.reshape(1, n))
    return pl.pallas_call(
        functools.partial(_matmul_kernel, silu_lhs=silu_lhs, has_bias=bias is not None),
        grid=(pl.cdiv(m, tm), pl.cdiv(n, tn)),
        in_specs=in_specs,
        out_specs=pl.BlockSpec((tm, tn), lambda i, j: (i, j)),
        out_shape=jax.ShapeDtypeStruct((m, n), out_dtype),
        compiler_params=_cparams("parallel", "arbitrary"),
        name=name,
    )(*args)


def _headwise_kernel(x_ref, w_ref, o_ref, *, transpose_w):
    x = x_ref[...].astype(BF16)
    w = w_ref[...].astype(BF16)
    dims = (((1,), (1,)), ((), ())) if transpose_w else (((1,), (0,)), ((), ()))
    o_ref[...] = lax.dot_general(x, w, dims, preferred_element_type=F32)


def headwise_matmul(x, w, *, heads, transpose_w, name):
    m = x.shape[0]
    dx = x.shape[1] // heads
    c = w.shape[0]
    dw = w.shape[1] // heads
    do = c if transpose_w else dw
    return pl.pallas_call(
        functools.partial(_headwise_kernel, transpose_w=transpose_w),
        grid=(heads,),
        in_specs=[pl.BlockSpec((m, dx), lambda h: (0, h)), pl.BlockSpec((c, dw), lambda h: (0, h))],
        out_specs=pl.BlockSpec((m, do), lambda h: (0, h)),
        out_shape=jax.ShapeDtypeStruct((m, heads * do), F32),
        compiler_params=_cparams("arbitrary"),
        name=name,
    )(x, w)


ROW_TILE = 128
EPS = 1e-6


def _table_row_block(i, tiles_per_prompt_batch, n_prompt_batches):
    return jnp.minimum(i // tiles_per_prompt_batch, n_prompt_batches)


def _modulate_kernel(x_ref, g_ref, *rest, modulated):
    o_ref = rest[-1]
    x = x_ref[...]
    y = x * lax.rsqrt(jnp.mean(x * x, axis=-1, keepdims=True) + EPS)
    y = y * g_ref[...]
    if modulated:
        shift_ref, scale_ref = rest[0], rest[1]
        y = y * (1.0 + scale_ref[...]) + shift_ref[...]
    o_ref[...] = y.astype(o_ref.dtype)


def modulate(x, gain, tables, k_shift, *, tiles_per_batch, n_batches, out_dtype, name):
    m, d = x.shape
    in_specs = [pl.BlockSpec((ROW_TILE, d), lambda i: (i, 0)), pl.BlockSpec((1, d), lambda i: (0, 0))]
    args = [x, gain.reshape(1, d)]
    if tables is not None:
        for k in (k_shift, k_shift + 1):
            in_specs.append(pl.BlockSpec(
                (None, ROW_TILE, d), lambda i, k=k: (k, _table_row_block(i, tiles_per_batch, n_batches), 0)))
            args.append(tables)
    return pl.pallas_call(
        functools.partial(_modulate_kernel, modulated=tables is not None),
        grid=(m // ROW_TILE,),
        in_specs=in_specs,
        out_specs=pl.BlockSpec((ROW_TILE, d), lambda i: (i, 0)),
        out_shape=jax.ShapeDtypeStruct((m, d), out_dtype),
        compiler_params=_cparams("parallel"),
        name=name,
    )(*args)


def _residual_kernel(x_ref, gate_ref, a_ref, *rest, two_terms):
    o_ref = rest[-1]
    a = a_ref[...]
    if two_terms:
        a2_ref, w1_ref, w2_ref = rest[0], rest[1], rest[2]
        a = w1_ref[...] * a + w2_ref[...] * a2_ref[...]
    o_ref[...] = x_ref[...] + gate_ref[...] * a


def residual(x, tables, k_gate, a, *, tiles_per_batch, n_batches, name, a2=None, w1=None, w2=None):
    m, d = x.shape
    row = pl.BlockSpec((ROW_TILE, d), lambda i: (i, 0))
    in_specs = [row, pl.BlockSpec((None, ROW_TILE, d),
                                  lambda i: (k_gate, _table_row_block(i, tiles_per_batch, n_batches), 0)), row]
    args = [x, tables, a]
    if a2 is not None:
        col = pl.BlockSpec((ROW_TILE, 1), lambda i: (i, 0))
        in_specs += [row, col, col]
        args += [a2, w1, w2]
    return pl.pallas_call(
        functools.partial(_residual_kernel, two_terms=a2 is not None),
        grid=(m // ROW_TILE,),
        in_specs=in_specs,
        out_specs=row,
        out_shape=jax.ShapeDtypeStruct((m, d), F32),
        compiler_params=_cparams("parallel"),
        name=name,
    )(*args)


def _swiglu_kernel(te_ref, tv_ref, x_ref, wg_ref, wu_ref, wd_ref, o_ref):
    t = pl.program_id(0)
    f = pl.program_id(1)

    @pl.when(f == 0)
    def _():
        o_ref[...] = jnp.zeros_like(o_ref)

    @pl.when(tv_ref[t] > 0)
    def _():
        x = x_ref[...].astype(BF16)
        g = jnp.dot(x, wg_ref[...].astype(BF16), preferred_element_type=F32)
        u = jnp.dot(x, wu_ref[...].astype(BF16), preferred_element_type=F32)
        a = (g * jax.nn.sigmoid(g) * u).astype(BF16)
        o_ref[...] += jnp.dot(a, wd_ref[...].astype(BF16), preferred_element_type=F32)


def grouped_swiglu(x, wg, wu, wd, tile_expert, tile_valid, *, tm, tf, name):
    m, d = x.shape
    f_total = wg.shape[2]
    nf = f_total // tf

    def f_eff(t, f, tv):
        return jnp.where(tv[t] > 0, f, nf - 1)

    grid_spec = pltpu.PrefetchScalarGridSpec(
        num_scalar_prefetch=2,
        grid=(m // tm, nf),
        in_specs=[pl.BlockSpec((tm, d), lambda t, f, te, tv: (t, 0)),
                  pl.BlockSpec((None, d, tf), lambda t, f, te, tv: (te[t], 0, f_eff(t, f, tv))),
                  pl.BlockSpec((None, d, tf), lambda t, f, te, tv: (te[t], 0, f_eff(t, f, tv))),
                  pl.BlockSpec((None, tf, d), lambda t, f, te, tv: (te[t], f_eff(t, f, tv), 0))],
        out_specs=pl.BlockSpec((tm, d), lambda t, f, te, tv: (t, 0)),
    )
    return pl.pallas_call(
        _swiglu_kernel,
        grid_spec=grid_spec,
        out_shape=jax.ShapeDtypeStruct((m, d), F32),
        compiler_params=_cparams("arbitrary", "arbitrary"),
        name=name,
    )(tile_expert, tile_valid, x, wg, wu, wd)


def _flash_kernel(q_ref, k_ref, v_ref, *rest, scale, use_bias):
    if use_bias:
        fq_ref, fk_ref = rest[0], rest[1]
        rest = rest[2:]
    o_ref, m_s, l_s, acc_s = rest
    qi = pl.program_id(2)
    ki = pl.program_id(3)

    @pl.when(ki == 0)
    def _():
        m_s[...] = jnp.full_like(m_s, -jnp.inf)
        l_s[...] = jnp.zeros_like(l_s)
        acc_s[...] = jnp.zeros_like(acc_s)

    @pl.when(ki <= qi)
    def _():
        q = q_ref[...]
        k = k_ref[...]
        s = lax.dot_general(q, k, (((1,), (1,)), ((), ())), preferred_element_type=F32) * scale
        if use_bias:
            s = s + fq_ref[...] - fk_ref[...]
        row = lax.broadcasted_iota(jnp.int32, s.shape, 0)
        col = lax.broadcasted_iota(jnp.int32, s.shape, 1)
        s = jnp.where(jnp.logical_and(ki == qi, col > row), -jnp.inf, s)
        m_prev = m_s[...]
        m_new = jnp.maximum(m_prev, jnp.max(s, axis=-1, keepdims=True))
        alpha = jnp.exp(m_prev - m_new)
        p = jnp.exp(s - m_new)
        l_s[...] = l_s[...] * alpha + jnp.sum(p, axis=-1, keepdims=True)
        acc_s[...] = acc_s[...] * alpha + jnp.dot(p.astype(BF16), v_ref[...], preferred_element_type=F32)
        m_s[...] = m_new

    @pl.when(ki == qi)
    def _():
        o_ref[...] = (acc_s[...] / l_s[...]).astype(o_ref.dtype)


def flash_attention(q, k, v, *, scale, tile, name, fq=None, fk=None):
    b, h, t, dk = q.shape
    hk = k.shape[1]
    dv = v.shape[3]
    g = h // hk
    nt = t // tile
    in_specs = [pl.BlockSpec((None, None, tile, dk), lambda b_, h_, i, j: (b_, h_, i, 0)),
                pl.BlockSpec((None, None, tile, dk), lambda b_, h_, i, j: (b_, h_ // g, jnp.minimum(i, j), 0)),
                pl.BlockSpec((None, None, tile, dv), lambda b_, h_, i, j: (b_, h_ // g, jnp.minimum(i, j), 0))]
    args = [q, k, v]
    if fq is not None:
        in_specs += [pl.BlockSpec((None, None, tile, 1), lambda b_, h_, i, j: (b_, h_, i, 0)),
                     pl.BlockSpec((None, None, 1, tile), lambda b_, h_, i, j: (b_, h_, 0, jnp.minimum(i, j)))]
        args += [fq, fk]
    return pl.pallas_call(
        functools.partial(_flash_kernel, scale=scale, use_bias=fq is not None),
        grid=(b, h, nt, nt),
        in_specs=in_specs,
        out_specs=pl.BlockSpec((None, tile, dv), lambda b_, h_, i, j: (b_, i, h_)),
        out_shape=jax.ShapeDtypeStruct((b, t, h * dv), F32),
        scratch_shapes=[pltpu.VMEM((tile, 1), F32), pltpu.VMEM((tile, 1), F32), pltpu.VMEM((tile, dv), F32)],
        compiler_params=_cparams("parallel", "parallel", "arbitrary", "arbitrary"),
        name=name,
    )(*args)


GDN_HEADS = 16
GDN_D = 128
GDN_CHUNK = 64
GDN_CONV = 4
GDN_HD = GDN_HEADS * GDN_D
GDN_CONV_CH = 3 * GDN_HD


def _dot_hi(a, b):
    a_hi = a.astype(BF16)
    a_lo = (a - a_hi.astype(F32)).astype(BF16)
    b_hi = b.astype(BF16)
    b_lo = (b - b_hi.astype(F32)).astype(BF16)
    return (jnp.dot(a_hi, b_hi, preferred_element_type=F32) + jnp.dot(a_hi, b_lo, preferred_element_type=F32)
            + jnp.dot(a_lo, b_hi, preferred_element_type=F32))


def _silu(x):
    return x * jax.nn.sigmoid(x)


def _gdn_prompt_kernel(qkv_ref, z_ref, ab_ref, cw_ref, alog_ref, dtb_ref, go_ref, lincl_ref,
                       og_ref, sfin_ref, state, xprev):
    n = pl.program_id(1)
    c = GDN_CHUNK

    @pl.when(n == 0)
    def _():
        state[...] = jnp.zeros_like(state)
        xprev[...] = jnp.zeros_like(xprev)

    x = qkv_ref[...]
    xp = xprev[...]
    trow = lax.broadcasted_iota(jnp.int32, (c, 1), 0)
    cw = cw_ref[...]
    y = x * cw[GDN_CONV - 1:GDN_CONV]
    for s in range(1, GDN_CONV):
        shifted = pltpu.roll(jnp.where(trow >= c - s, xp, x), s, axis=0)
        y = y + shifted * cw[GDN_CONV - 1 - s:GDN_CONV - s]
    xprev[...] = x
    y = _silu(y)

    ab = ab_ref[...]
    a_in = ab[:, :GDN_HEADS] + dtb_ref[...]
    softplus = jnp.maximum(a_in, 0.0) + jnp.log1p(jnp.exp(-jnp.abs(a_in)))
    gate = -jnp.exp(alog_ref[...]) * softplus
    beta = jax.nn.sigmoid(ab[:, GDN_HEADS:2 * GDN_HEADS])

    lincl = lincl_ref[...]
    g3 = _split3(gate)
    gc_cols = sum(jnp.dot(lincl, t, preferred_element_type=F32) for t in g3)
    gc_rows = sum(lax.dot_general(t, lincl, (((0,), (1,)), ((), ())), preferred_element_type=F32) for t in g3)
    row = lax.broadcasted_iota(jnp.int32, (c, c), 0)
    col = lax.broadcasted_iota(jnp.int32, (c, c), 1)
    incl = row >= col
    strict = row > col
    eye = (row == col).astype(F32)
    right_half = lax.broadcasted_iota(jnp.int32, (c, 2 * c), 1) >= c

    heads = range(GDN_HEADS)
    qbs, kbs, khs, vhs, decays, gccs, bcols, zmats = [], [], [], [], [], [], [], []
    for h in heads:
        qh = y[:, h * GDN_D:(h + 1) * GDN_D]
        kh = y[:, GDN_HD + h * GDN_D:GDN_HD + (h + 1) * GDN_D]
        qh = qh * lax.rsqrt(jnp.sum(qh * qh, axis=-1, keepdims=True) + EPS) * (GDN_D ** -0.5)
        kh = kh * lax.rsqrt(jnp.sum(kh * kh, axis=-1, keepdims=True) + EPS)
        gcc = gc_cols[:, h:h + 1]
        bcol = beta[:, h:h + 1]
        decay = jnp.exp(jnp.where(incl, gcc - gc_rows[h:h + 1, :], -jnp.inf))
        kb = kh.astype(BF16)
        kk = lax.dot_general(kb, kb, (((1,), (1,)), ((), ())), preferred_element_type=F32)
        nmat = -(jnp.where(strict, decay, 0.0) * bcol * kk)
        zmats.append(jnp.concatenate([_dot_hi(nmat, nmat), eye + nmat], axis=1))
        qbs.append(qh.astype(BF16)), kbs.append(kb), khs.append(kh), decays.append(decay)
        vhs.append(y[:, 2 * GDN_HD + h * GDN_D:2 * GDN_HD + (h + 1) * GDN_D]), gccs.append(gcc), bcols.append(bcol)
    for _ in range(4):
        zmats = [_dot_hi(z[:, :c], z) + jnp.where(right_half, z, 0.0) for z in zmats]
    wus = []
    for h in heads:
        z = zmats[h]
        tinv = z[:, c:] + _dot_hi(z[:, :c], z[:, c:])
        rhs = jnp.concatenate([(bcols[h] * jnp.exp(gccs[h])) * khs[h], bcols[h] * vhs[h]], axis=1)
        wus.append(_dot_hi(tinv, rhs))
    for h in heads:
        sl = slice(h * GDN_D, (h + 1) * GDN_D)
        gcc = gccs[h]
        w, u = wus[h][:, :GDN_D], wus[h][:, GDN_D:]
        qk = lax.dot_general(qbs[h], kbs[h], (((1,), (1,)), ((), ())), preferred_element_type=F32) * decays[h]
        s_old = state[h]
        sb = s_old.astype(BF16)
        wq_s = jnp.dot(jnp.concatenate([w.astype(BF16), qbs[h]], axis=0), sb, preferred_element_type=F32)
        ub = (u - wq_s[:c]).astype(BF16)
        o = jnp.exp(gcc) * wq_s[c:] + jnp.dot(qk.astype(BF16), ub, preferred_element_type=F32)
        glast = gcc[c - 1:c, :]
        k_end = (khs[h] * jnp.exp(glast - gcc)).astype(BF16)
        state[h] = jnp.exp(glast) * s_old + lax.dot_general(k_end, ub, (((0,), (0,)), ((), ())),
                                                            preferred_element_type=F32)
        o = o * lax.rsqrt(jnp.mean(o * o, axis=-1, keepdims=True) + EPS) * go_ref[...]
        og_ref[:, sl] = (o * _silu(z_ref[:, sl])).astype(og_ref.dtype)

    @pl.when(n == pl.num_programs(1) - 1)
    def _():
        sfin_ref[...] = state[...]


def gdn_prompt(p1, nb, t, conv_w, a_log, dt_bias, g_o):
    c = GDN_CHUNK
    nchunk = t // c
    lincl = jnp.asarray(np.tril(np.ones((c, c), np.float32)), BF16)
    row_block = lambda b_, n: b_ * nchunk + n
    gate_col = (GDN_CONV_CH + GDN_HD) // 128
    small = lambda shape: pl.BlockSpec(shape, lambda b_, n: (0, 0))
    return pl.pallas_call(
        _gdn_prompt_kernel,
        grid=(nb, nchunk),
        in_specs=[pl.BlockSpec((c, GDN_CONV_CH), lambda b_, n: (row_block(b_, n), 0)),
                  pl.BlockSpec((c, GDN_HD), lambda b_, n: (row_block(b_, n), GDN_CONV_CH // GDN_HD)),
                  pl.BlockSpec((c, 128), lambda b_, n: (row_block(b_, n), gate_col)),
                  small((GDN_CONV, GDN_CONV_CH)), small((1, GDN_HEADS)), small((1, GDN_HEADS)), small((1, GDN_D)),
                  small((c, c))],
        out_specs=[pl.BlockSpec((c, GDN_HD), lambda b_, n: (row_block(b_, n), 0)),
                   pl.BlockSpec((None, GDN_HEADS, GDN_D, GDN_D), lambda b_, n: (b_, 0, 0, 0))],
        out_shape=[jax.ShapeDtypeStruct((nb * t, GDN_HD), BF16),
                   jax.ShapeDtypeStruct((nb, GDN_HEADS, GDN_D, GDN_D), F32)],
        scratch_shapes=[pltpu.VMEM((GDN_HEADS, GDN_D, GDN_D), F32), pltpu.VMEM((c, GDN_CONV_CH), F32)],
        compiler_params=_cparams("parallel", "arbitrary"),
        name="gdn_prompt",
    )(p1, p1, p1, conv_w.reshape(GDN_CONV, GDN_CONV_CH), a_log.reshape(1, GDN_HEADS),
      dt_bias.reshape(1, GDN_HEADS), g_o.reshape(1, GDN_D), lincl)


def _gdn_step_kernel(q_ref, k_ref, kcol_ref, v_ref, g_ref, beta_ref, s_ref, o_ref, snew_ref):
    eg = jnp.exp(g_ref[...])
    for h in range(GDN_HEADS):
        sl = slice(h * GDN_D, (h + 1) * GDN_D)
        s1 = s_ref[h] * eg[:, h:h + 1]
        k8 = jnp.broadcast_to(k_ref[:, sl], (8, GDN_D)).astype(BF16)
        ks = jnp.dot(k8, s1.astype(BF16), preferred_element_type=F32)[0:1]
        u = beta_ref[:, h:h + 1] * (v_ref[:, sl] - ks)
        s2 = s1 + kcol_ref[h] * u
        q8 = jnp.broadcast_to(q_ref[:, sl], (8, GDN_D)).astype(BF16)
        o_ref[:, sl] = jnp.dot(q8, s2.astype(BF16), preferred_element_type=F32)[0:1]
        snew_ref[h] = s2


def gdn_step(q, k, v, g, beta, s0):
    b = q.shape[0]
    kcol = k.reshape(b, GDN_HEADS, GDN_D, 1)
    vec = pl.BlockSpec((None, 1, GDN_HD), lambda i: (i, 0, 0))
    gate = pl.BlockSpec((None, 1, GDN_HEADS), lambda i: (i, 0, 0))
    st = pl.BlockSpec((None, GDN_HEADS, GDN_D, GDN_D), lambda i: (i, 0, 0, 0))
    return pl.pallas_call(
        _gdn_step_kernel,
        grid=(b,),
        in_specs=[vec, vec, pl.BlockSpec((None, GDN_HEADS, GDN_D, 1), lambda i: (i, 0, 0, 0)), vec, gate, gate, st],
        out_specs=[vec, st],
        out_shape=[jax.ShapeDtypeStruct((b, 1, GDN_HD), F32), jax.ShapeDtypeStruct(s0.shape, F32)],
        compiler_params=_cparams("parallel"),
        name="gdn_step",
    )(q, k, kcol, v, g, beta, s0)


ROPE_THETA = 10000.0
MLA_Q_LORA = 512
N_MOD = 6
TOP_K = 2
MATMUL_ROWS = 1040
MATMUL_COLS = 512
MOE_ROWS = 1024
FF_COLS = 256
FLASH_TILE = 1024


def _pick_tile(m, target):
    best = None
    for t in range(16, min(m, target) + 1, 16):
        if m % t == 0:
            best = t
    return best if best is not None else m


def _rmsnorm(x, gain):
    y = x * lax.rsqrt(jnp.mean(x * x, axis=-1, keepdims=True) + EPS)
    return y * gain


def _rope(x, cos, sin):
    half = x.shape[-1] // 2
    x1, x2 = x[..., :half], x[..., half:]
    c, s = cos[:, None, :], sin[:, None, :]
    return jnp.concatenate([x1 * c - x2 * s, x1 * s + x2 * c], axis=-1)


def _route(top_idx, n_experts, tm):
    m = top_idx.shape[0]
    flat_e = top_idx.reshape(-1)
    onehot = (flat_e[:, None] == jnp.arange(n_experts, dtype=jnp.int32)[None, :]).astype(jnp.int32)
    rank = jnp.sum((jnp.cumsum(onehot, axis=0) - onehot) * onehot, axis=1)
    counts = jnp.sum(onehot, axis=0)
    tiles_e = (counts + tm - 1) // tm
    tile_end = jnp.cumsum(tiles_e)
    tile_start = tile_end - tiles_e
    pos = tile_start[flat_e] * tm + rank
    n_tiles = (TOP_K * m + n_experts * (tm - 1)) // tm
    src = jnp.zeros((n_tiles * tm,), jnp.int32).at[pos].set(jnp.arange(TOP_K * m, dtype=jnp.int32) // TOP_K)
    t = jnp.arange(n_tiles, dtype=jnp.int32)
    tile_valid = (t < tile_end[-1]).astype(jnp.int32)
    t_clamped = jnp.minimum(t, tile_end[-1] - 1)
    tile_expert = jnp.sum((t_clamped[:, None] >= tile_end[None, :]).astype(jnp.int32), axis=1)
    return src, pos.reshape(m, TOP_K), tile_expert.astype(jnp.int32), tile_valid


def kernel(x_prompt, x_sample, cache_mla_ckv, cache_mla_krope, cache_fox_k, cache_fox_v, cache_fox_logf, state_gdn, state_gdn_conv, page_table, c_prompt, c_sample, l0_w_mod, l0_b_mod, l0_g_mix, l0_g_ffn, l0_w_in, l0_g_qa, l0_w_uq, l0_g_kva, l0_w_uk, l0_w_uv, l0_b_f, l0_w_out, l0_w1, l0_w3, l0_w2, l1_w_mod, l1_b_mod, l1_g_mix, l1_g_ffn, l1_w_in, l1_conv_w, l1_a_log, l1_dt_bias, l1_g_o, l1_w_out, l1_w_router, l1_b_router, l1_wg, l1_wu, l1_wd, g_final):
    nb, t, d = x_prompt.shape
    db = x_sample.shape[0]
    assert x_sample.shape[1] == 1 and db == ROW_TILE and t % ROW_TILE == 0
    mp = nb * t
    m = mp + db
    past = page_table.shape[1] * cache_mla_ckv.shape[1]
    tpb = t // ROW_TILE
    tm = _pick_tile(m, MATMUL_ROWS)
    tmp = _pick_tile(mp, MATMUL_ROWS)
    mm = functools.partial(matmul, tn=MATMUL_COLS)
    row_kw = dict(tiles_per_batch=tpb, n_batches=nb)

    x = jnp.concatenate([x_prompt.reshape(mp, d), x_sample.reshape(db, d)], axis=0)
    c_all = jnp.concatenate([c_prompt, c_sample], axis=0)

    def mod_tables(w_mod, b_mod, name):
        mod = matmul(c_all, w_mod, tm=nb + db, tn=1024, bias=b_mod, silu_lhs=True, name=name)
        mod = jnp.swapaxes(mod.reshape(nb + db, N_MOD, d), 0, 1)
        return jnp.concatenate([jnp.repeat(mod[:, :nb], ROW_TILE, axis=1), mod[:, nb:]], axis=1)

    pos = jnp.concatenate([jnp.tile(jnp.arange(t, dtype=jnp.int32), nb), jnp.full((db,), past, jnp.int32)])
    half = MLA_ROPE // 2
    inv = ROPE_THETA ** (-jnp.arange(half, dtype=F32) / half)
    ang = pos.astype(F32)[:, None] * inv[None, :]
    cos, sin = jnp.cos(ang), jnp.sin(ang)

    t0 = mod_tables(l0_w_mod, l0_b_mod, "adaln0")
    h = modulate(x, l0_g_mix, t0, 0, out_dtype=BF16, name="mod_mix0", **row_kw)
    p0 = mm(h, l0_w_in, tm=tm, name="proj0")
    nq, nc, nr = MLA_Q_LORA, MLA_KV_LORA, MLA_ROPE
    nfq, nfk = FOX_HEADS * FOX_DIM, FOX_KV_HEADS * FOX_DIM
    offs = np.cumsum([0, nq, nc, nr, nfq, nfk, nfk, FOX_HEADS])
    q_lat, kv_lat, k_r, fq, fk, fv, f_logit = (p0[:, offs[i]:offs[i + 1]] for i in range(7))
    w_uq = l0_w_uq.reshape(nq, MLA_HEADS, MLA_NOPE + MLA_ROPE)
    w_q = jnp.concatenate([w_uq[:, :, :MLA_NOPE].reshape(nq, -1), w_uq[:, :, MLA_NOPE:].reshape(nq, -1)], axis=1)
    qq = mm(_rmsnorm(q_lat, l0_g_qa).astype(BF16), w_q, tm=tm, name="q_up")
    q_nope = qq[:, :MLA_HEADS * MLA_NOPE]
    q_rope = _rope(qq[:, MLA_HEADS * MLA_NOPE:].reshape(m, MLA_HEADS, MLA_ROPE), cos, sin)
    c_kv = _rmsnorm(kv_lat, l0_g_kva)
    k_rope = _rope(k_r[:, None, :], cos, sin)[:, 0]
    lf = jax.nn.log_sigmoid(f_logit + l0_b_f)

    w_uk2 = l0_w_uk.reshape(nc, MLA_HEADS * MLA_NOPE)
    w_uv2 = l0_w_uv.reshape(nc, MLA_HEADS * MLA_NOPE)
    kv = mm(c_kv[:mp].astype(BF16), jnp.concatenate([w_uk2, w_uv2], axis=1), tm=tmp, name="kv_up")

    def heads_first(a, n_heads):
        return jnp.swapaxes(a.reshape(nb, t, n_heads, -1), 1, 2).astype(BF16)

    q_cat = jnp.concatenate([q_nope[:mp].reshape(nb, t, MLA_HEADS, MLA_NOPE),
                             q_rope[:mp].reshape(nb, t, MLA_HEADS, MLA_ROPE)], axis=-1)
    k_cat = jnp.concatenate([kv[:, :MLA_HEADS * MLA_NOPE].reshape(nb, t, MLA_HEADS, MLA_NOPE),
                             jnp.broadcast_to(k_rope[:mp].reshape(nb, t, 1, MLA_ROPE), (nb, t, MLA_HEADS, MLA_ROPE))],
                            axis=-1)
    ftile = _pick_tile(t, FLASH_TILE)
    o_mla_p = flash_attention(heads_first(q_cat, MLA_HEADS), heads_first(k_cat, MLA_HEADS),
                              heads_first(kv[:, MLA_HEADS * MLA_NOPE:], MLA_HEADS),
                              scale=MLA_SCALE, tile=ftile, name="flash_mla")
    fcum = jnp.swapaxes(jnp.cumsum(lf[:mp].reshape(nb, t, FOX_HEADS), axis=1), 1, 2)
    o_fox_p = flash_attention(heads_first(fq[:mp], FOX_HEADS), heads_first(fk[:mp], FOX_KV_HEADS),
                              heads_first(fv[:mp], FOX_KV_HEADS), scale=FOX_SCALE, tile=ftile, name="flash_fox",
                              fq=fcum[..., None], fk=fcum[:, :, None, :])

    q_abs = headwise_matmul(q_nope[mp:], w_uk2, heads=MLA_HEADS, transpose_w=True, name="q_absorb")
    o_lat, o_fox_s = decode_attention(
        page_table, q_abs.reshape(db, MLA_HEADS, nc), q_rope[mp:], c_kv[mp:, None, :], k_rope[mp:, None, :],
        fq[mp:].reshape(db, FOX_HEADS, FOX_DIM), fk[mp:].reshape(db, FOX_KV_HEADS, FOX_DIM),
        fv[mp:].reshape(db, FOX_KV_HEADS, FOX_DIM), lf[mp:, :, None],
        cache_mla_ckv, cache_mla_krope, cache_fox_k, cache_fox_v, cache_fox_logf)
    o_mla_s = headwise_matmul(o_lat.reshape(db, MLA_HEADS * nc), w_uv2, heads=MLA_HEADS, transpose_w=False,
                              name="o_unabsorb")

    o = jnp.concatenate([jnp.concatenate([o_mla_p, o_fox_p], axis=-1).reshape(mp, -1),
                         jnp.concatenate([o_mla_s, o_fox_s.reshape(db, -1)], axis=-1)], axis=0).astype(BF16)
    x = residual(x, t0, 2, mm(o, l0_w_out, tm=tm, name="out0"), name="res_mix0", **row_kw)
    h = modulate(x, l0_g_ffn, t0, 3, out_dtype=BF16, name="mod_ffn0", **row_kw)
    n_row_tiles = m // tm
    ffn = grouped_swiglu(h, l0_w1[None], l0_w3[None], l0_w2[None], jnp.zeros((n_row_tiles,), jnp.int32),
                         jnp.ones((n_row_tiles,), jnp.int32), tm=tm, tf=FF_COLS, name="ffn0")
    x = residual(x, t0, 5, ffn, name="res_ffn0", **row_kw)

    t1 = mod_tables(l1_w_mod, l1_b_mod, "adaln1")
    h = modulate(x, l1_g_mix, t1, 0, out_dtype=BF16, name="mod_mix1", **row_kw)
    p1 = mm(h, l1_w_in, tm=tm, name="proj1")
    og_p, p_gdn = gdn_prompt(p1, nb, t, l1_conv_w, l1_a_log, l1_dt_bias, l1_g_o)
    ps = p1[mp:]
    cw = l1_conv_w.reshape(GDN_CONV, GDN_CONV_CH)
    xin_s = jnp.concatenate([state_gdn_conv, ps[:, None, :GDN_CONV_CH]], axis=1)
    y_s = _silu(sum(xin_s[:, w] * cw[w] for w in range(GDN_CONV)))

    def l2n(a):
        a = a.reshape(db, GDN_HEADS, GDN_D)
        return (a * lax.rsqrt(jnp.sum(a * a, axis=-1, keepdims=True) + EPS)).reshape(db, 1, GDN_HD)

    z_s = ps[:, GDN_CONV_CH:GDN_CONV_CH + GDN_HD]
    a_s = ps[:, GDN_CONV_CH + GDN_HD:GDN_CONV_CH + GDN_HD + GDN_HEADS]
    b_s = ps[:, GDN_CONV_CH + GDN_HD + GDN_HEADS:GDN_CONV_CH + GDN_HD + 2 * GDN_HEADS]
    g_s = -jnp.exp(l1_a_log) * jax.nn.softplus(a_s + l1_dt_bias)
    o_s, s_gdn = gdn_step(l2n(y_s[:, :GDN_HD]) * (GDN_D ** -0.5), l2n(y_s[:, GDN_HD:2 * GDN_HD]),
                          y_s[:, None, 2 * GDN_HD:], g_s[:, None, :], jax.nn.sigmoid(b_s)[:, None, :], state_gdn)
    og_s = _rmsnorm(o_s.reshape(db, GDN_HEADS, GDN_D), l1_g_o) * _silu(z_s.reshape(db, GDN_HEADS, GDN_D))
    og = jnp.concatenate([og_p, og_s.reshape(db, GDN_HD).astype(BF16)], axis=0)
    x = residual(x, t1, 2, mm(og, l1_w_out, tm=tm, name="out1"), name="res_mix1", **row_kw)
    h = modulate(x, l1_g_ffn, t1, 3, out_dtype=F32, name="mod_ffn1", **row_kw)
    n_exp = l1_w_router.shape[1]
    logits = matmul(h, l1_w_router, tm=tm, tn=n_exp, bias=l1_b_router, name="router")
    top_val, top_idx = lax.top_k(logits, TOP_K)
    top_w = jax.nn.softmax(top_val, axis=-1)
    src, slot, tile_expert, tile_valid = _route(top_idx.astype(jnp.int32), n_exp, MOE_ROWS)
    take_rows = lambda a, idx: a.at[idx].get(mode="promise_in_bounds")
    ys = grouped_swiglu(take_rows(h, src), l1_wg, l1_wu, l1_wd, tile_expert, tile_valid,
                        tm=MOE_ROWS, tf=FF_COLS, name="moe")
    x = residual(x, t1, 5, take_rows(ys, slot[:, 0]), name="res_moe",
                 a2=take_rows(ys, slot[:, 1]), w1=top_w[:, 0:1], w2=top_w[:, 1:2], **row_kw)
    y_out = modulate(x, g_final, None, 0, out_dtype=F32, name="final_norm", **row_kw)

    def split(a, *tail):
        return a[:mp].reshape((nb, t) + tail), a[mp:].reshape((db, 1) + tail)

    y_prompt, y_sample = split(y_out, d)
    p_ckv, s_ckv = split(c_kv, nc)
    p_krope, s_krope = split(k_rope, nr)
    p_fk, s_fk = split(fk, FOX_KV_HEADS, FOX_DIM)
    p_fv, s_fv = split(fv, FOX_KV_HEADS, FOX_DIM)
    p_lf, s_lf = split(lf, FOX_HEADS)
    p_conv = p1[:mp, :GDN_CONV_CH].reshape(nb, t, GDN_CONV_CH)[:, t - (GDN_CONV - 1):]
    s_conv = xin_s[:, 1:]
    return (y_prompt, y_sample, p_ckv, p_krope, p_fk, p_fv, p_lf, p_gdn, p_conv,
            s_ckv, s_krope, s_fk, s_fv, s_lf, s_gdn, s_conv)
```

```python
import functools

import jax
import jax.numpy as jnp
import numpy as np
from jax import lax
from jax.experimental import pallas as pl
from jax.experimental.pallas import tpu as pltpu

D_MODEL = 2048
MLA_HEADS = 8
MLA_NOPE = 128
MLA_ROPE = 64
MLA_KV_LORA = 256
FOX_HEADS = 8
FOX_KV_HEADS = 2
FOX_DIM = 128
PAGE_SIZE = 128
MLA_SCALE = (MLA_NOPE + MLA_ROPE) ** -0.5
FOX_SCALE = FOX_DIM ** -0.5

BF16 = jnp.bfloat16
F32 = jnp.float32
VMEM_LIMIT = 56 * 1024 * 1024


def _cparams(*sem):
    return pltpu.CompilerParams(dimension_semantics=sem, vmem_limit_bytes=VMEM_LIMIT)


def _split3(x):
    hi = x.astype(BF16)
    r = x - hi.astype(F32)
    mid = r.astype(BF16)
    lo = (r - mid.astype(F32)).astype(BF16)
    return hi, mid, lo


DEC_PAGES_PER_STEP = 32
DEC_BATCH_PER_STEP = 1
N_CACHES = 5


def _decode_kernel(pt_ref, qa_ref, qr_ref, cn_ref, rn_ref, fq_ref, fkn_ref, fvn_ref, lfn_ref, tri_ref,
                   ckv_hbm, krt_hbm, fk_hbm, fv_hbm, lft_hbm, olat_ref, ofox_ref,
                   bc, br, bk, bv, bl, sems, m1, l1, a1, m2, l2, a2, csum, *, n_pages):
    pc, nbb = DEC_PAGES_PER_STEP, DEC_BATCH_PER_STEP
    nc = n_pages // pc
    total = (qa_ref.shape[0] // nbb) * nc
    caches = ((ckv_hbm, bc), (krt_hbm, br), (fk_hbm, bk), (fv_hbm, bv), (lft_hbm, bl))

    def page_copies(step, slot, for_wait):
        first_b = (step // nc) * nbb
        j = step % nc
        out = []
        for bb in range(nbb):
            for k in range(pc):
                page = 0 if for_wait else pt_ref[first_b + bb, n_pages - 1 - (j * pc + k)]
                for i, (src, dst) in enumerate(caches):
                    out.append(pltpu.make_async_copy(src.at[page], dst.at[slot, bb, k], sems.at[slot, i]))
        return out

    head = lax.broadcasted_iota(jnp.int32, (FOX_HEADS, 2 * PAGE_SIZE), 0)
    col = lax.broadcasted_iota(jnp.int32, (FOX_HEADS, 2 * PAGE_SIZE), 1)
    own = (col % FOX_KV_HEADS) == (head // (FOX_HEADS // FOX_KV_HEADS))
    g = FOX_HEADS // FOX_KV_HEADS

    def step_fn(step, carry):
        slot = step % 2
        first_b = (step // nc) * nbb
        j = step % nc

        @pl.when(step + 1 < total)
        def _():
            for c in page_copies(step + 1, 1 - slot, False):
                c.start()

        for c in page_copies(step, slot, True):
            c.wait()

        @pl.when(j == 0)
        def _():
            for bb in range(nbb):
                b = first_b + bb
                qa = qa_ref[b].astype(BF16).astype(F32)
                qr = qr_ref[b].astype(BF16).astype(F32)
                cn = cn_ref[b].astype(BF16).astype(F32)
                rn = rn_ref[b].astype(BF16).astype(F32)
                m1[bb] = (jnp.sum(qa * cn, axis=-1, keepdims=True)
                          + jnp.sum(qr * rn, axis=-1, keepdims=True)) * MLA_SCALE
                l1[bb] = jnp.ones((MLA_HEADS, 1), F32)
                a1[bb] = jnp.broadcast_to(cn, (MLA_HEADS, MLA_KV_LORA))
                fq = fq_ref[b].astype(BF16).astype(F32)
                fkn = fkn_ref[b].astype(BF16).astype(F32)
                fvn = fvn_ref[b].astype(BF16).astype(F32)
                kn_h = jnp.concatenate([jnp.broadcast_to(fkn[i:i + 1], (g, FOX_DIM)) for i in range(FOX_KV_HEADS)], 0)
                vn_h = jnp.concatenate([jnp.broadcast_to(fvn[i:i + 1], (g, FOX_DIM)) for i in range(FOX_KV_HEADS)], 0)
                m2[bb] = jnp.sum(fq * kn_h, axis=-1, keepdims=True) * FOX_SCALE
                l2[bb] = jnp.ones((FOX_HEADS, 1), F32)
                a2[bb] = vn_h
                csum[bb] = jnp.broadcast_to(lfn_ref[b], (FOX_HEADS, 2 * PAGE_SIZE))

        lf_rows = jnp.concatenate([bl[slot, bb, k] for bb in range(nbb) for k in range(pc)], axis=0)
        n_rows = lf_rows.shape[0]
        suf3 = jnp.dot(jnp.concatenate(_split3(lf_rows), axis=0), tri_ref[...], preferred_element_type=F32)
        suf = suf3[:n_rows] + suf3[n_rows:2 * n_rows] + suf3[2 * n_rows:]

        for bb in range(nbb):
            b = first_b + bb
            qa = qa_ref[b].astype(BF16)
            qr = qr_ref[b].astype(BF16)
            cbs = [bc[slot, bb, k].astype(BF16) for k in range(pc)]
            s_list = []
            for k in range(pc):
                rbt = br[slot, bb, k].astype(BF16)
                s = lax.dot_general(qa, cbs[k], (((1,), (1,)), ((), ())), preferred_element_type=F32)
                s = s + jnp.dot(qr, rbt, preferred_element_type=F32)
                s_list.append(s * MLA_SCALE)
            m_prev = m1[bb]
            m_new = m_prev
            for s in s_list:
                m_new = jnp.maximum(m_new, jnp.max(s, axis=-1, keepdims=True))
            alpha = jnp.exp(m_prev - m_new)
            l_new = l1[bb] * alpha
            acc = a1[bb] * alpha
            for k in range(pc):
                p = jnp.exp(s_list[k] - m_new)
                l_new = l_new + jnp.sum(p, axis=-1, keepdims=True)
                acc = acc + jnp.dot(p.astype(BF16), cbs[k], preferred_element_type=F32)
            m1[bb] = m_new
            l1[bb] = l_new
            a1[bb] = acc

            fq = fq_ref[b].astype(BF16)
            c_run = csum[bb]
            s_list = []
            vbs = []
            for k in range(pc):
                r0 = (bb * pc + k) * FOX_HEADS
                bias = c_run + suf[r0:r0 + FOX_HEADS, :2 * PAGE_SIZE]
                c_run = c_run + suf[r0:r0 + FOX_HEADS, 2 * PAGE_SIZE:]
                kb = bk[slot, bb, k].astype(BF16)
                vbs.append(bv[slot, bb, k].astype(BF16))
                s = lax.dot_general(fq, kb, (((1,), (1,)), ((), ())), preferred_element_type=F32) * FOX_SCALE + bias
                s_list.append(jnp.where(own, s, -jnp.inf))
            csum[bb] = c_run
            m_prev = m2[bb]
            m_new = m_prev
            for s in s_list:
                m_new = jnp.maximum(m_new, jnp.max(s, axis=-1, keepdims=True))
            alpha = jnp.exp(m_prev - m_new)
            l_new = l2[bb] * alpha
            acc = a2[bb] * alpha
            for k in range(pc):
                p = jnp.exp(s_list[k] - m_new)
                l_new = l_new + jnp.sum(p, axis=-1, keepdims=True)
                acc = acc + jnp.dot(p.astype(BF16), vbs[k], preferred_element_type=F32)
            m2[bb] = m_new
            l2[bb] = l_new
            a2[bb] = acc

        @pl.when(j == nc - 1)
        def _():
            for bb in range(nbb):
                olat_ref[first_b + bb] = a1[bb] / l1[bb]
                ofox_ref[first_b + bb] = a2[bb] / l2[bb]

        return carry

    for c in page_copies(0, 0, False):
        c.start()
    lax.fori_loop(0, total, step_fn, 0)


def _suffix_matrix():
    j = np.arange(PAGE_SIZE)[:, None]
    s = np.arange(2 * PAGE_SIZE)[None, :] // FOX_KV_HEADS
    strict = (j > s).astype(np.float32)
    return jnp.asarray(np.concatenate([strict, np.ones((PAGE_SIZE, 2 * PAGE_SIZE), np.float32)], axis=1), BF16)


def decode_attention(page_table, q_abs, q_rope, c_new, r_new, fq, fk_new, fv_new, lf_new,
                     cache_ckv, cache_krope, cache_fox_k, cache_fox_v, cache_logf):
    db, n_pages = page_table.shape
    pc, nbb = DEC_PAGES_PER_STEP, DEC_BATCH_PER_STEP
    assert db % nbb == 0 and n_pages % pc == 0
    n_pool = cache_ckv.shape[0]
    fk2 = cache_fox_k.reshape(n_pool, PAGE_SIZE * FOX_KV_HEADS, FOX_DIM)
    fv2 = cache_fox_v.reshape(n_pool, PAGE_SIZE * FOX_KV_HEADS, FOX_DIM)
    krope_t = jnp.swapaxes(cache_krope, 1, 2)
    logf_t = jnp.swapaxes(cache_logf, 1, 2)
    vmem = pl.BlockSpec(memory_space=pltpu.VMEM)
    hbm = pl.BlockSpec(memory_space=pl.ANY)
    slots = (2, nbb, pc)
    return pl.pallas_call(
        functools.partial(_decode_kernel, n_pages=n_pages),
        in_specs=[pl.BlockSpec(memory_space=pltpu.SMEM)] + [vmem] * 9 + [hbm] * N_CACHES,
        out_specs=[vmem, vmem],
        out_shape=[jax.ShapeDtypeStruct((db, MLA_HEADS, MLA_KV_LORA), F32),
                   jax.ShapeDtypeStruct((db, FOX_HEADS, FOX_DIM), F32)],
        scratch_shapes=[pltpu.VMEM(slots + (PAGE_SIZE, MLA_KV_LORA), F32),
                        pltpu.VMEM(slots + (MLA_ROPE, PAGE_SIZE), F32),
                        pltpu.VMEM(slots + (2 * PAGE_SIZE, FOX_DIM), F32),
                        pltpu.VMEM(slots + (2 * PAGE_SIZE, FOX_DIM), F32),
                        pltpu.VMEM(slots + (FOX_HEADS, PAGE_SIZE), F32),
                        pltpu.SemaphoreType.DMA((2, N_CACHES)),
                        pltpu.VMEM((nbb, MLA_HEADS, 1), F32), pltpu.VMEM((nbb, MLA_HEADS, 1), F32),
                        pltpu.VMEM((nbb, MLA_HEADS, MLA_KV_LORA), F32),
                        pltpu.VMEM((nbb, FOX_HEADS, 1), F32), pltpu.VMEM((nbb, FOX_HEADS, 1), F32),
                        pltpu.VMEM((nbb, FOX_HEADS, FOX_DIM), F32),
                        pltpu.VMEM((nbb, FOX_HEADS, 2 * PAGE_SIZE), F32)],
        compiler_params=pltpu.CompilerParams(vmem_limit_bytes=VMEM_LIMIT),
        name="decode_attention",
    )(page_table, q_abs, q_rope, c_new, r_new, fq, fk_new, fv_new, lf_new, _suffix_matrix(),
      cache_ckv, krope_t, fk2, fv2, logf_t)


def _matmul_kernel(x_ref, w_ref, *rest, silu_lhs, has_bias):
    o_ref = rest[-1]
    x = x_ref[...]
    if silu_lhs:
        x = x.astype(F32)
        x = x * jax.nn.sigmoid(x)
    acc = jnp.dot(x.astype(BF16), w_ref[...].astype(BF16), preferred_element_type=F32)
    if has_bias:
        acc = acc + rest[0][...]
    o_ref[...] = acc.astype(o_ref.dtype)


def matmul(x, w, *, tm, tn, name, out_dtype=F32, bias=None, silu_lhs=False):
    m, k = x.shape
    n = w.shape[1]
    tm = min(tm, m)
    tn = min(tn, n)
    in_specs = [pl.BlockSpec((tm, k), lambda i, j: (i, 0)), pl.BlockSpec((k, tn), lambda i, j: (0, j))]
    args = [x, w]
    if bias is not None:
        in_specs.append(pl.BlockSpec((1, tn), lambda i, j: (0, j)))
        args.append(bias.reshape(1, n))
    return pl.pallas_call(
        functools.partial(_matmul_kernel, silu_lhs=silu_lhs, has_bias=bias is not None),
        grid=(pl.cdiv(m, tm), pl.cdiv(n, tn)),
        in_specs=in_specs,
        out_specs=pl.BlockSpec((tm, tn), lambda i, j: (i, j)),
        out_shape=jax.ShapeDtypeStruct((m, n), out_dtype),
        compiler_params=_cparams("parallel", "arbitrary"),
        name=name,
    )(*args)


def _headwise_kernel(x_ref, w_ref, o_ref, *, transpose_w):
    x = x_ref[...].astype(BF16)
    w = w_ref[...].astype(BF16)
    dims = (((1,), (1,)), ((), ())) if transpose_w else (((1,), (0,)), ((), ()))
    o_ref[...] = lax.dot_general(x, w, dims, preferred_element_type=F32)


def headwise_matmul(x, w, *, heads, transpose_w, name):
    m = x.shape[0]
    dx = x.shape[1] // heads
    c = w.shape[0]
    dw = w.shape[1] // heads
    do = c if transpose_w else dw
    return pl.pallas_call(
        functools.partial(_headwise_kernel, transpose_w=transpose_w),
        grid=(heads,),
        in_specs=[pl.BlockSpec((m, dx), lambda h: (0, h)), pl.BlockSpec((c, dw), lambda h: (0, h))],
        out_specs=pl.BlockSpec((m, do), lambda h: (0, h)),
        out_shape=jax.ShapeDtypeStruct((m, heads * do), F32),
        compiler_params=_cparams("arbitrary"),
        name=name,
    )(x, w)


ROW_TILE = 128
EPS = 1e-6


def _table_row_block(i, tiles_per_prompt_batch, n_prompt_batches):
    return jnp.minimum(i // tiles_per_prompt_batch, n_prompt_batches)


def _modulate_kernel(x_ref, g_ref, *rest, modulated):
    o_ref = rest[-1]
    x = x_ref[...]
    y = x * lax.rsqrt(jnp.mean(x * x, axis=-1, keepdims=True) + EPS)
    y = y * g_ref[...]
    if modulated:
        shift_ref, scale_ref = rest[0], rest[1]
        y = y * (1.0 + scale_ref[...]) + shift_ref[...]
    o_ref[...] = y.astype(o_ref.dtype)


def modulate(x, gain, tables, k_shift, *, tiles_per_batch, n_batches, out_dtype, name):
    m, d = x.shape
    in_specs = [pl.BlockSpec((ROW_TILE, d), lambda i: (i, 0)), pl.BlockSpec((1, d), lambda i: (0, 0))]
    args = [x, gain.reshape(1, d)]
    if tables is not None:
        for k in (k_shift, k_shift + 1):
            in_specs.append(pl.BlockSpec(
                (None, ROW_TILE, d), lambda i, k=k: (k, _table_row_block(i, tiles_per_batch, n_batches), 0)))
            args.append(tables)
    return pl.pallas_call(
        functools.partial(_modulate_kernel, modulated=tables is not None),
        grid=(m // ROW_TILE,),
        in_specs=in_specs,
        out_specs=pl.BlockSpec((ROW_TILE, d), lambda i: (i, 0)),
        out_shape=jax.ShapeDtypeStruct((m, d), out_dtype),
        compiler_params=_cparams("parallel"),
        name=name,
    )(*args)


def _residual_modulate_kernel(x_ref, gate_ref, a_ref, *rest, two_terms, modulated, emit_x):
    rest = list(rest)
    a = a_ref[...]
    if two_terms:
        a2_ref, w_ref = rest[:2]
        rest = rest[2:]
        n = a.shape[0]
        diag = lax.broadcasted_iota(jnp.int32, (n, n), 0) == lax.broadcasted_iota(jnp.int32, (n, n), 1)
        w = w_ref[...]
        w1 = jnp.sum(jnp.where(diag, w[0:1, :], 0.0), axis=1, keepdims=True)
        w2 = jnp.sum(jnp.where(diag, w[1:2, :], 0.0), axis=1, keepdims=True)
        a = w1 * a + w2 * a2_ref[...]
    g_ref = rest.pop(0)
    if modulated:
        shift_ref, scale_ref = rest[:2]
        rest = rest[2:]
    x = x_ref[...] + gate_ref[...] * a
    if emit_x:
        rest.pop(0)[...] = x
    y = x * lax.rsqrt(jnp.mean(x * x, axis=-1, keepdims=True) + EPS)
    y = y * g_ref[...]
    if modulated:
        y = y * (1.0 + scale_ref[...]) + shift_ref[...]
    rest[0][...] = y.astype(rest[0].dtype)


def residual_modulate(x, gate_tables, k_gate, a, gain, mod_tables, k_shift, *, tiles_per_batch, n_batches,
                      out_dtype, name, a2=None, w=None, emit_x=True):
    m, d = x.shape
    row = pl.BlockSpec((ROW_TILE, d), lambda i: (i, 0))

    def table(k):
        return pl.BlockSpec((None, ROW_TILE, d), lambda i: (k, _table_row_block(i, tiles_per_batch, n_batches), 0))

    in_specs = [row, table(k_gate), row]
    args = [x, gate_tables, a]
    if a2 is not None:
        in_specs += [row, pl.BlockSpec((TOP_K, ROW_TILE), lambda i: (0, i))]
        args += [a2, w]
    in_specs.append(pl.BlockSpec((1, d), lambda i: (0, 0)))
    args.append(gain.reshape(1, d))
    if mod_tables is not None:
        in_specs += [table(k_shift), table(k_shift + 1)]
        args += [mod_tables, mod_tables]
    out_shape = [jax.ShapeDtypeStruct((m, d), out_dtype)]
    if emit_x:
        out_shape.insert(0, jax.ShapeDtypeStruct((m, d), F32))
    out = pl.pallas_call(
        functools.partial(_residual_modulate_kernel, two_terms=a2 is not None, modulated=mod_tables is not None,
                          emit_x=emit_x),
        grid=(m // ROW_TILE,),
        in_specs=in_specs,
        out_specs=[row] * len(out_shape),
        out_shape=out_shape,
        compiler_params=_cparams("parallel"),
        name=name,
    )(*args)
    return out if emit_x else out[0]


TOP_K = 2


def _router_kernel(h_ref, wrt_ref, b_ref, before_ref, idx_ref, w_ref, rank_ref, cnt_ref, carry):
    @pl.when(pl.program_id(0) == 0)
    def _():
        carry[...] = jnp.zeros_like(carry)

    logits = lax.dot_general(wrt_ref[...].astype(BF16), h_ref[...].astype(BF16), (((1,), (1,)), ((), ())),
                             preferred_element_type=F32) + b_ref[...]
    n_exp = logits.shape[0]
    e_iota = lax.broadcasted_iota(jnp.int32, logits.shape, 0)
    m1 = jnp.max(logits, axis=0, keepdims=True)
    i1 = jnp.min(jnp.where(logits == m1, e_iota, n_exp), axis=0, keepdims=True)
    rest = jnp.where(e_iota == i1, -jnp.inf, logits)
    m2 = jnp.max(rest, axis=0, keepdims=True)
    i2 = jnp.min(jnp.where(rest == m2, e_iota, n_exp), axis=0, keepdims=True)
    ex = jnp.exp(m2 - m1)
    w_ref[...] = jnp.concatenate([1.0 / (1.0 + ex), ex / (1.0 + ex)], axis=0)
    idx_ref[...] = jnp.concatenate([i1, i2], axis=0)
    oh1 = (e_iota == i1).astype(F32)
    oh2 = (e_iota == i2).astype(F32)
    both = oh1 + oh2
    earlier = carry[...] + jnp.dot(both.astype(BF16), before_ref[...], preferred_element_type=F32)
    rank_ref[...] = jnp.concatenate([jnp.sum(oh1 * earlier, axis=0, keepdims=True),
                                     jnp.sum(oh2 * earlier, axis=0, keepdims=True)], axis=0).astype(jnp.int32)
    carry[...] = carry[...] + jnp.sum(both, axis=1, keepdims=True)
    cnt_ref[...] = jnp.broadcast_to(carry[...], cnt_ref.shape).astype(jnp.int32)


def router(h, w_router, b_router):
    m, d = h.shape
    n_exp = w_router.shape[1]
    before = jnp.asarray(np.triu(np.ones((ROW_TILE, ROW_TILE), np.float32), 1), BF16)
    pair = pl.BlockSpec((TOP_K, ROW_TILE), lambda i: (0, i))
    idx, w, rank, cnt = pl.pallas_call(
        _router_kernel,
        grid=(m // ROW_TILE,),
        in_specs=[pl.BlockSpec((ROW_TILE, d), lambda i: (i, 0)), pl.BlockSpec((n_exp, d), lambda i: (0, 0)),
                  pl.BlockSpec((n_exp, 1), lambda i: (0, 0)), pl.BlockSpec((ROW_TILE, ROW_TILE), lambda i: (0, 0))],
        out_specs=[pair, pair, pair, pl.BlockSpec((n_exp, ROW_TILE), lambda i: (0, 0))],
        out_shape=[jax.ShapeDtypeStruct((TOP_K, m), jnp.int32), jax.ShapeDtypeStruct((TOP_K, m), F32),
                   jax.ShapeDtypeStruct((TOP_K, m), jnp.int32), jax.ShapeDtypeStruct((n_exp, ROW_TILE), jnp.int32)],
        scratch_shapes=[pltpu.VMEM((n_exp, 1), F32)],
        compiler_params=_cparams("arbitrary"),
        name="router",
    )(h, w_router.T, b_router.reshape(n_exp, 1), before)
    return idx, w, rank, cnt[:, 0]


GATHER_STEPS = 8


def _swiglu_kernel(te_ref, tv_ref, *rest, gather, tm, n_tiles):
    t = pl.program_id(0)
    f = pl.program_id(1)
    if gather:
        src_ref, x_hbm, wg_ref, wu_ref, wd_ref, o_ref, xg, xb, sem = rest
    else:
        x_ref, wg_ref, wu_ref, wd_ref, o_ref = rest

    @pl.when(f == 0)
    def _():
        o_ref[...] = jnp.zeros_like(o_ref)

    if gather:
        def row_copy(tok, r, slot):
            return pltpu.make_async_copy(x_hbm.at[pl.ds(tok, 1)], xg.at[slot, pl.ds(r, 1)], sem.at[slot])

        def issue(tile, r0, n):
            def body(i, carry):
                row_copy(src_ref[tile * tm + r0 + i], r0 + i, tile % 2).start()
                return carry
            lax.fori_loop(0, n, body, 0, unroll=8)

        @pl.when(jnp.logical_and(t == 0, f == 0))
        def _():
            issue(0, 0, tm)

        nxt = jnp.minimum(t + 1, n_tiles - 1)

        @pl.when(jnp.logical_and(jnp.logical_and(f < GATHER_STEPS, t + 1 < n_tiles), tv_ref[nxt] > 0))
        def _():
            issue(nxt, f * (tm // GATHER_STEPS), tm // GATHER_STEPS)

        @pl.when(jnp.logical_and(f == 0, tv_ref[t] > 0))
        def _():
            def body(i, carry):
                row_copy(0, i, t % 2).wait()
                return carry
            lax.fori_loop(0, tm, body, 0, unroll=8)
            xb[...] = xg[t % 2].astype(BF16)

    @pl.when(tv_ref[t] > 0)
    def _():
        x = xb[...] if gather else x_ref[...]
        g = jnp.dot(x, wg_ref[...].astype(BF16), preferred_element_type=F32)
        u = jnp.dot(x, wu_ref[...].astype(BF16), preferred_element_type=F32)
        a = (g * jax.nn.sigmoid(g) * u).astype(BF16)
        o_ref[...] += jnp.dot(a, wd_ref[...].astype(BF16), preferred_element_type=F32)


def grouped_swiglu(x, wg, wu, wd, tile_expert, tile_valid, *, tm, tf, name, src=None):
    d = x.shape[1]
    gather = src is not None
    n_rows = src.shape[0] if gather else x.shape[0]
    n_tiles = n_rows // tm
    nf = wg.shape[2] // tf
    assert tm % GATHER_STEPS == 0 and nf >= GATHER_STEPS

    def f_eff(t, f, tv):
        return jnp.where(tv[t] > 0, f, nf - 1)

    w_specs = [pl.BlockSpec((None, d, tf), lambda t, f, te, tv, *_: (te[t], 0, f_eff(t, f, tv))),
               pl.BlockSpec((None, d, tf), lambda t, f, te, tv, *_: (te[t], 0, f_eff(t, f, tv))),
               pl.BlockSpec((None, tf, d), lambda t, f, te, tv, *_: (te[t], f_eff(t, f, tv), 0))]
    if gather:
        x_spec = pl.BlockSpec(memory_space=pl.ANY)
        scratch = [pltpu.VMEM((2, tm, d), F32), pltpu.VMEM((tm, d), BF16), pltpu.SemaphoreType.DMA((2,))]
        prefetch = (tile_expert, tile_valid, src)
    else:
        x_spec = pl.BlockSpec((tm, d), lambda t, f, te, tv: (t, 0))
        scratch = []
        prefetch = (tile_expert, tile_valid)
    grid_spec = pltpu.PrefetchScalarGridSpec(
        num_scalar_prefetch=len(prefetch),
        grid=(n_tiles, nf),
        in_specs=[x_spec] + w_specs,
        out_specs=pl.BlockSpec((tm, d), lambda t, f, te, tv, *_: (t, 0)),
        scratch_shapes=scratch,
    )
    return pl.pallas_call(
        functools.partial(_swiglu_kernel, gather=gather, tm=tm, n_tiles=n_tiles),
        grid_spec=grid_spec,
        out_shape=jax.ShapeDtypeStruct((n_rows, d), F32),
        compiler_params=_cparams("arbitrary", "arbitrary"),
        name=name,
    )(*prefetch, x, wg, wu, wd)


def _flash_kernel(q_ref, k_ref, v_ref, *rest, scale, use_bias):
    if use_bias:
        fq_ref, fk_ref = rest[0], rest[1]
        rest = rest[2:]
    o_ref, m_s, l_s, acc_s = rest
    qi = pl.program_id(2)
    ki = pl.program_id(3)

    @pl.when(ki == 0)
    def _():
        m_s[...] = jnp.full_like(m_s, -jnp.inf)
        l_s[...] = jnp.zeros_like(l_s)
        acc_s[...] = jnp.zeros_like(acc_s)

    @pl.when(ki <= qi)
    def _():
        q = q_ref[...]
        k = k_ref[...]
        s = lax.dot_general(q, k, (((1,), (1,)), ((), ())), preferred_element_type=F32) * scale
        if use_bias:
            s = s + fq_ref[...] - fk_ref[...]
        row = lax.broadcasted_iota(jnp.int32, s.shape, 0)
        col = lax.broadcasted_iota(jnp.int32, s.shape, 1)
        s = jnp.where(jnp.logical_and(ki == qi, col > row), -jnp.inf, s)
        m_prev = m_s[...]
        m_new = jnp.maximum(m_prev, jnp.max(s, axis=-1, keepdims=True))
        alpha = jnp.exp(m_prev - m_new)
        p = jnp.exp(s - m_new)
        l_s[...] = l_s[...] * alpha + jnp.sum(p, axis=-1, keepdims=True)
        acc_s[...] = acc_s[...] * alpha + jnp.dot(p.astype(BF16), v_ref[...], preferred_element_type=F32)
        m_s[...] = m_new

    @pl.when(ki == qi)
    def _():
        o_ref[...] = (acc_s[...] / l_s[...]).astype(o_ref.dtype)


def flash_attention(q, k, v, *, scale, tile, name, fq=None, fk=None):
    b, h, t, dk = q.shape
    hk = k.shape[1]
    dv = v.shape[3]
    g = h // hk
    nt = t // tile
    in_specs = [pl.BlockSpec((None, None, tile, dk), lambda b_, h_, i, j: (b_, h_, i, 0)),
                pl.BlockSpec((None, None, tile, dk), lambda b_, h_, i, j: (b_, h_ // g, jnp.minimum(i, j), 0)),
                pl.BlockSpec((None, None, tile, dv), lambda b_, h_, i, j: (b_, h_ // g, jnp.minimum(i, j), 0))]
    args = [q, k, v]
    if fq is not None:
        in_specs += [pl.BlockSpec((None, None, tile, 1), lambda b_, h_, i, j: (b_, h_, i, 0)),
                     pl.BlockSpec((None, None, 1, tile), lambda b_, h_, i, j: (b_, h_, 0, jnp.minimum(i, j)))]
        args += [fq, fk]
    return pl.pallas_call(
        functools.partial(_flash_kernel, scale=scale, use_bias=fq is not None),
        grid=(b, h, nt, nt),
        in_specs=in_specs,
        out_specs=pl.BlockSpec((None, tile, dv), lambda b_, h_, i, j: (b_, i, h_)),
        out_shape=jax.ShapeDtypeStruct((b, t, h * dv), F32),
        scratch_shapes=[pltpu.VMEM((tile, 1), F32), pltpu.VMEM((tile, 1), F32), pltpu.VMEM((tile, dv), F32)],
        compiler_params=_cparams("parallel", "parallel", "arbitrary", "arbitrary"),
        name=name,
    )(*args)


GDN_HEADS = 16
GDN_D = 128
GDN_CHUNK = 64
GDN_CONV = 4
GDN_HD = GDN_HEADS * GDN_D
GDN_CONV_CH = 3 * GDN_HD


def _dot_hi(a, b):
    a_hi = a.astype(BF16)
    a_lo = (a - a_hi.astype(F32)).astype(BF16)
    b_hi = b.astype(BF16)
    b_lo = (b - b_hi.astype(F32)).astype(BF16)
    return (jnp.dot(a_hi, b_hi, preferred_element_type=F32) + jnp.dot(a_hi, b_lo, preferred_element_type=F32)
            + jnp.dot(a_lo, b_hi, preferred_element_type=F32))


def _silu(x):
    return x * jax.nn.sigmoid(x)


def _gdn_prompt_kernel(qkv_ref, z_ref, ab_ref, cw_ref, alog_ref, dtb_ref, go_ref, lincl_ref,
                       og_ref, sfin_ref, state, xprev):
    n = pl.program_id(1)
    c = GDN_CHUNK

    @pl.when(n == 0)
    def _():
        state[...] = jnp.zeros_like(state)
        xprev[...] = jnp.zeros_like(xprev)

    x = qkv_ref[...]
    xp = xprev[...]
    trow = lax.broadcasted_iota(jnp.int32, (c, 1), 0)
    cw = cw_ref[...]
    y = x * cw[GDN_CONV - 1:GDN_CONV]
    for s in range(1, GDN_CONV):
        shifted = pltpu.roll(jnp.where(trow >= c - s, xp, x), s, axis=0)
        y = y + shifted * cw[GDN_CONV - 1 - s:GDN_CONV - s]
    xprev[...] = x
    y = _silu(y)

    ab = ab_ref[...]
    a_in = ab[:, :GDN_HEADS] + dtb_ref[...]
    softplus = jnp.maximum(a_in, 0.0) + jnp.log1p(jnp.exp(-jnp.abs(a_in)))
    gate = -jnp.exp(alog_ref[...]) * softplus
    beta = jax.nn.sigmoid(ab[:, GDN_HEADS:2 * GDN_HEADS])

    lincl = lincl_ref[...]
    g3 = _split3(gate)
    gc_cols = sum(jnp.dot(lincl, t, preferred_element_type=F32) for t in g3)
    gc_rows = sum(lax.dot_general(t, lincl, (((0,), (1,)), ((), ())), preferred_element_type=F32) for t in g3)
    row = lax.broadcasted_iota(jnp.int32, (c, c), 0)
    col = lax.broadcasted_iota(jnp.int32, (c, c), 1)
    incl = row >= col
    strict = row > col
    eye = (row == col).astype(F32)
    right_half = lax.broadcasted_iota(jnp.int32, (c, 2 * c), 1) >= c

    heads = range(GDN_HEADS)
    qbs, kbs, khs, vhs, decays, gccs, bcols, zmats = [], [], [], [], [], [], [], []
    for h in heads:
        qh = y[:, h * GDN_D:(h + 1) * GDN_D]
        kh = y[:, GDN_HD + h * GDN_D:GDN_HD + (h + 1) * GDN_D]
        qh = qh * lax.rsqrt(jnp.sum(qh * qh, axis=-1, keepdims=True) + EPS) * (GDN_D ** -0.5)
        kh = kh * lax.rsqrt(jnp.sum(kh * kh, axis=-1, keepdims=True) + EPS)
        gcc = gc_cols[:, h:h + 1]
        bcol = beta[:, h:h + 1]
        decay = jnp.exp(jnp.where(incl, gcc - gc_rows[h:h + 1, :], -jnp.inf))
        kb = kh.astype(BF16)
        kk = lax.dot_general(kb, kb, (((1,), (1,)), ((), ())), preferred_element_type=F32)
        nmat = -(jnp.where(strict, decay, 0.0) * bcol * kk)
        zmats.append(jnp.concatenate([_dot_hi(nmat, nmat), eye + nmat], axis=1))
        qbs.append(qh.astype(BF16)), kbs.append(kb), khs.append(kh), decays.append(decay)
        vhs.append(y[:, 2 * GDN_HD + h * GDN_D:2 * GDN_HD + (h + 1) * GDN_D]), gccs.append(gcc), bcols.append(bcol)
    for _ in range(4):
        zmats = [_dot_hi(z[:, :c], z) + jnp.where(right_half, z, 0.0) for z in zmats]
    wus = []
    for h in heads:
        z = zmats[h]
        tinv = z[:, c:] + _dot_hi(z[:, :c], z[:, c:])
        rhs = jnp.concatenate([(bcols[h] * jnp.exp(gccs[h])) * khs[h], bcols[h] * vhs[h]], axis=1)
        wus.append(_dot_hi(tinv, rhs))
    for h in heads:
        sl = slice(h * GDN_D, (h + 1) * GDN_D)
        gcc = gccs[h]
        w, u = wus[h][:, :GDN_D], wus[h][:, GDN_D:]
        qk = lax.dot_general(qbs[h], kbs[h], (((1,), (1,)), ((), ())), preferred_element_type=F32) * decays[h]
        s_old = state[h]
        sb = s_old.astype(BF16)
        wq_s = jnp.dot(jnp.concatenate([w.astype(BF16), qbs[h]], axis=0), sb, preferred_element_type=F32)
        ub = (u - wq_s[:c]).astype(BF16)
        o = jnp.exp(gcc) * wq_s[c:] + jnp.dot(qk.astype(BF16), ub, preferred_element_type=F32)
        glast = gcc[c - 1:c, :]
        k_end = (khs[h] * jnp.exp(glast - gcc)).astype(BF16)
        state[h] = jnp.exp(glast) * s_old + lax.dot_general(k_end, ub, (((0,), (0,)), ((), ())),
                                                            preferred_element_type=F32)
        o = o * lax.rsqrt(jnp.mean(o * o, axis=-1, keepdims=True) + EPS) * go_ref[...]
        og_ref[:, sl] = (o * _silu(z_ref[:, sl])).astype(og_ref.dtype)

    @pl.when(n == pl.num_programs(1) - 1)
    def _():
        sfin_ref[...] = state[...]


def gdn_prompt(p1, nb, t, conv_w, a_log, dt_bias, g_o):
    c = GDN_CHUNK
    nchunk = t // c
    lincl = jnp.asarray(np.tril(np.ones((c, c), np.float32)), BF16)
    row_block = lambda b_, n: b_ * nchunk + n
    gate_col = (GDN_CONV_CH + GDN_HD) // 128
    small = lambda shape: pl.BlockSpec(shape, lambda b_, n: (0, 0))
    return pl.pallas_call(
        _gdn_prompt_kernel,
        grid=(nb, nchunk),
        in_specs=[pl.BlockSpec((c, GDN_CONV_CH), lambda b_, n: (row_block(b_, n), 0)),
                  pl.BlockSpec((c, GDN_HD), lambda b_, n: (row_block(b_, n), GDN_CONV_CH // GDN_HD)),
                  pl.BlockSpec((c, 128), lambda b_, n: (row_block(b_, n), gate_col)),
                  small((GDN_CONV, GDN_CONV_CH)), small((1, GDN_HEADS)), small((1, GDN_HEADS)), small((1, GDN_D)),
                  small((c, c))],
        out_specs=[pl.BlockSpec((c, GDN_HD), lambda b_, n: (row_block(b_, n), 0)),
                   pl.BlockSpec((None, GDN_HEADS, GDN_D, GDN_D), lambda b_, n: (b_, 0, 0, 0))],
        out_shape=[jax.ShapeDtypeStruct((nb * t, GDN_HD), BF16),
                   jax.ShapeDtypeStruct((nb, GDN_HEADS, GDN_D, GDN_D), F32)],
        scratch_shapes=[pltpu.VMEM((GDN_HEADS, GDN_D, GDN_D), F32), pltpu.VMEM((c, GDN_CONV_CH), F32)],
        compiler_params=_cparams("parallel", "arbitrary"),
        name="gdn_prompt",
    )(p1, p1, p1, conv_w.reshape(GDN_CONV, GDN_CONV_CH), a_log.reshape(1, GDN_HEADS),
      dt_bias.reshape(1, GDN_HEADS), g_o.reshape(1, GDN_D), lincl)


def _gdn_step_kernel(q_ref, k_ref, kcol_ref, v_ref, g_ref, beta_ref, s_ref, o_ref, snew_ref):
    eg = jnp.exp(g_ref[...])
    for h in range(GDN_HEADS):
        sl = slice(h * GDN_D, (h + 1) * GDN_D)
        s1 = s_ref[h] * eg[:, h:h + 1]
        k8 = jnp.broadcast_to(k_ref[:, sl], (8, GDN_D)).astype(BF16)
        ks = jnp.dot(k8, s1.astype(BF16), preferred_element_type=F32)[0:1]
        u = beta_ref[:, h:h + 1] * (v_ref[:, sl] - ks)
        s2 = s1 + kcol_ref[h] * u
        q8 = jnp.broadcast_to(q_ref[:, sl], (8, GDN_D)).astype(BF16)
        o_ref[:, sl] = jnp.dot(q8, s2.astype(BF16), preferred_element_type=F32)[0:1]
        snew_ref[h] = s2


def gdn_step(q, k, v, g, beta, s0):
    b = q.shape[0]
    kcol = k.reshape(b, GDN_HEADS, GDN_D, 1)
    vec = pl.BlockSpec((None, 1, GDN_HD), lambda i: (i, 0, 0))
    gate = pl.BlockSpec((None, 1, GDN_HEADS), lambda i: (i, 0, 0))
    st = pl.BlockSpec((None, GDN_HEADS, GDN_D, GDN_D), lambda i: (i, 0, 0, 0))
    return pl.pallas_call(
        _gdn_step_kernel,
        grid=(b,),
        in_specs=[vec, vec, pl.BlockSpec((None, GDN_HEADS, GDN_D, 1), lambda i: (i, 0, 0, 0)), vec, gate, gate, st],
        out_specs=[vec, st],
        out_shape=[jax.ShapeDtypeStruct((b, 1, GDN_HD), F32), jax.ShapeDtypeStruct(s0.shape, F32)],
        compiler_params=_cparams("parallel"),
        name="gdn_step",
    )(q, k, kcol, v, g, beta, s0)


ROPE_THETA = 10000.0
MLA_Q_LORA = 512
N_MOD = 6
MATMUL_ROWS = 1040
MATMUL_COLS = 512
MOE_ROWS = 1024
FF_COLS = 256
FLASH_TILE = 1024


def _pick_tile(m, target):
    best = None
    for t in range(16, min(m, target) + 1, 16):
        if m % t == 0:
            best = t
    return best if best is not None else m


def _rmsnorm(x, gain):
    y = x * lax.rsqrt(jnp.mean(x * x, axis=-1, keepdims=True) + EPS)
    return y * gain


def _rope(x, cos, sin):
    half = x.shape[-1] // 2
    x1, x2 = x[..., :half], x[..., half:]
    c, s = cos[:, None, :], sin[:, None, :]
    return jnp.concatenate([x1 * c - x2 * s, x1 * s + x2 * c], axis=-1)


def _route(idx, rank, counts, tm):
    n_experts = counts.shape[0]
    m = idx.shape[1]
    tiles_e = (counts + tm - 1) // tm
    tile_end = jnp.cumsum(tiles_e)
    tile_start = tile_end - tiles_e
    pos = jnp.take(tile_start, idx) * tm + rank
    n_tiles = (TOP_K * m + n_experts * (tm - 1)) // tm
    tok = jnp.broadcast_to(jnp.arange(m, dtype=jnp.int32)[None, :], (TOP_K, m))
    src = jnp.zeros((n_tiles * tm,), jnp.int32).at[pos.reshape(-1)].set(tok.reshape(-1))
    t = jnp.arange(n_tiles, dtype=jnp.int32)
    tile_valid = (t < tile_end[-1]).astype(jnp.int32)
    t_clamped = jnp.minimum(t, tile_end[-1] - 1)
    tile_expert = jnp.sum((t_clamped[:, None] >= tile_end[None, :]).astype(jnp.int32), axis=1)
    return src, pos, tile_expert.astype(jnp.int32), tile_valid


def kernel(x_prompt, x_sample, cache_mla_ckv, cache_mla_krope, cache_fox_k, cache_fox_v, cache_fox_logf, state_gdn, state_gdn_conv, page_table, c_prompt, c_sample, l0_w_mod, l0_b_mod, l0_g_mix, l0_g_ffn, l0_w_in, l0_g_qa, l0_w_uq, l0_g_kva, l0_w_uk, l0_w_uv, l0_b_f, l0_w_out, l0_w1, l0_w3, l0_w2, l1_w_mod, l1_b_mod, l1_g_mix, l1_g_ffn, l1_w_in, l1_conv_w, l1_a_log, l1_dt_bias, l1_g_o, l1_w_out, l1_w_router, l1_b_router, l1_wg, l1_wu, l1_wd, g_final):
    nb, t, d = x_prompt.shape
    db = x_sample.shape[0]
    assert x_sample.shape[1] == 1 and db == ROW_TILE and t % ROW_TILE == 0
    mp = nb * t
    m = mp + db
    past = page_table.shape[1] * cache_mla_ckv.shape[1]
    tpb = t // ROW_TILE
    tm = _pick_tile(m, MATMUL_ROWS)
    tmp = _pick_tile(mp, MATMUL_ROWS)
    mm = functools.partial(matmul, tn=MATMUL_COLS)
    row_kw = dict(tiles_per_batch=tpb, n_batches=nb)

    x = jnp.concatenate([x_prompt.reshape(mp, d), x_sample.reshape(db, d)], axis=0)
    c_all = jnp.concatenate([c_prompt, c_sample], axis=0)

    def mod_tables(w_mod, b_mod, name):
        mod = matmul(c_all, w_mod, tm=nb + db, tn=1024, bias=b_mod, silu_lhs=True, name=name)
        mod = jnp.swapaxes(mod.reshape(nb + db, N_MOD, d), 0, 1)
        return jnp.concatenate([jnp.repeat(mod[:, :nb], ROW_TILE, axis=1), mod[:, nb:]], axis=1)

    pos = jnp.concatenate([jnp.tile(jnp.arange(t, dtype=jnp.int32), nb), jnp.full((db,), past, jnp.int32)])
    half = MLA_ROPE // 2
    inv = ROPE_THETA ** (-jnp.arange(half, dtype=F32) / half)
    ang = pos.astype(F32)[:, None] * inv[None, :]
    cos, sin = jnp.cos(ang), jnp.sin(ang)

    t0 = mod_tables(l0_w_mod, l0_b_mod, "adaln0")
    h = modulate(x, l0_g_mix, t0, 0, out_dtype=BF16, name="mod_mix0", **row_kw)
    p0 = mm(h, l0_w_in, tm=tm, name="proj0")
    nq, nc, nr = MLA_Q_LORA, MLA_KV_LORA, MLA_ROPE
    nfq, nfk = FOX_HEADS * FOX_DIM, FOX_KV_HEADS * FOX_DIM
    offs = np.cumsum([0, nq, nc, nr, nfq, nfk, nfk, FOX_HEADS])
    q_lat, kv_lat, k_r, fq, fk, fv, f_logit = (p0[:, offs[i]:offs[i + 1]] for i in range(7))
    w_uq = l0_w_uq.reshape(nq, MLA_HEADS, MLA_NOPE + MLA_ROPE)
    w_q = jnp.concatenate([w_uq[:, :, :MLA_NOPE].reshape(nq, -1), w_uq[:, :, MLA_NOPE:].reshape(nq, -1)], axis=1)
    qq = mm(_rmsnorm(q_lat, l0_g_qa).astype(BF16), w_q, tm=tm, name="q_up")
    q_nope = qq[:, :MLA_HEADS * MLA_NOPE]
    q_rope = _rope(qq[:, MLA_HEADS * MLA_NOPE:].reshape(m, MLA_HEADS, MLA_ROPE), cos, sin)
    c_kv = _rmsnorm(kv_lat, l0_g_kva)
    k_rope = _rope(k_r[:, None, :], cos, sin)[:, 0]
    lf = jax.nn.log_sigmoid(f_logit + l0_b_f)

    w_uk2 = l0_w_uk.reshape(nc, MLA_HEADS * MLA_NOPE)
    w_uv2 = l0_w_uv.reshape(nc, MLA_HEADS * MLA_NOPE)
    kv = mm(c_kv[:mp].astype(BF16), jnp.concatenate([w_uk2, w_uv2], axis=1), tm=tmp, name="kv_up")

    def heads_first(a, n_heads):
        return jnp.swapaxes(a.reshape(nb, t, n_heads, -1), 1, 2).astype(BF16)

    q_cat = jnp.concatenate([q_nope[:mp].reshape(nb, t, MLA_HEADS, MLA_NOPE),
                             q_rope[:mp].reshape(nb, t, MLA_HEADS, MLA_ROPE)], axis=-1)
    k_cat = jnp.concatenate([kv[:, :MLA_HEADS * MLA_NOPE].reshape(nb, t, MLA_HEADS, MLA_NOPE),
                             jnp.broadcast_to(k_rope[:mp].reshape(nb, t, 1, MLA_ROPE), (nb, t, MLA_HEADS, MLA_ROPE))],
                            axis=-1)
    ftile = _pick_tile(t, FLASH_TILE)
    o_mla_p = flash_attention(heads_first(q_cat, MLA_HEADS), heads_first(k_cat, MLA_HEADS),
                              heads_first(kv[:, MLA_HEADS * MLA_NOPE:], MLA_HEADS),
                              scale=MLA_SCALE, tile=ftile, name="flash_mla")
    fcum = jnp.swapaxes(jnp.cumsum(lf[:mp].reshape(nb, t, FOX_HEADS), axis=1), 1, 2)
    o_fox_p = flash_attention(heads_first(fq[:mp], FOX_HEADS), heads_first(fk[:mp], FOX_KV_HEADS),
                              heads_first(fv[:mp], FOX_KV_HEADS), scale=FOX_SCALE, tile=ftile, name="flash_fox",
                              fq=fcum[..., None], fk=fcum[:, :, None, :])

    q_abs = headwise_matmul(q_nope[mp:], w_uk2, heads=MLA_HEADS, transpose_w=True, name="q_absorb")
    o_lat, o_fox_s = decode_attention(
        page_table, q_abs.reshape(db, MLA_HEADS, nc), q_rope[mp:], c_kv[mp:, None, :], k_rope[mp:, None, :],
        fq[mp:].reshape(db, FOX_HEADS, FOX_DIM), fk[mp:].reshape(db, FOX_KV_HEADS, FOX_DIM),
        fv[mp:].reshape(db, FOX_KV_HEADS, FOX_DIM), lf[mp:, :, None],
        cache_mla_ckv, cache_mla_krope, cache_fox_k, cache_fox_v, cache_fox_logf)
    o_mla_s = headwise_matmul(o_lat.reshape(db, MLA_HEADS * nc), w_uv2, heads=MLA_HEADS, transpose_w=False,
                              name="o_unabsorb")

    o = jnp.concatenate([jnp.concatenate([o_mla_p, o_fox_p], axis=-1).reshape(mp, -1),
                         jnp.concatenate([o_mla_s, o_fox_s.reshape(db, -1)], axis=-1)], axis=0).astype(BF16)
    x, h = residual_modulate(x, t0, 2, mm(o, l0_w_out, tm=tm, name="out0"), l0_g_ffn, t0, 3, out_dtype=BF16,
                             name="res_mix0_mod_ffn0", **row_kw)
    n_row_tiles = m // tm
    ffn = grouped_swiglu(h, l0_w1[None], l0_w3[None], l0_w2[None], jnp.zeros((n_row_tiles,), jnp.int32),
                         jnp.ones((n_row_tiles,), jnp.int32), tm=tm, tf=FF_COLS, name="ffn0")

    t1 = mod_tables(l1_w_mod, l1_b_mod, "adaln1")
    x, h = residual_modulate(x, t0, 5, ffn, l1_g_mix, t1, 0, out_dtype=BF16, name="res_ffn0_mod_mix1", **row_kw)
    p1 = mm(h, l1_w_in, tm=tm, name="proj1")
    og_p, p_gdn = gdn_prompt(p1, nb, t, l1_conv_w, l1_a_log, l1_dt_bias, l1_g_o)
    ps = p1[mp:]
    cw = l1_conv_w.reshape(GDN_CONV, GDN_CONV_CH)
    xin_s = jnp.concatenate([state_gdn_conv, ps[:, None, :GDN_CONV_CH]], axis=1)
    y_s = _silu(sum(xin_s[:, w] * cw[w] for w in range(GDN_CONV)))

    def l2n(a):
        a = a.reshape(db, GDN_HEADS, GDN_D)
        return (a * lax.rsqrt(jnp.sum(a * a, axis=-1, keepdims=True) + EPS)).reshape(db, 1, GDN_HD)

    z_s = ps[:, GDN_CONV_CH:GDN_CONV_CH + GDN_HD]
    a_s = ps[:, GDN_CONV_CH + GDN_HD:GDN_CONV_CH + GDN_HD + GDN_HEADS]
    b_s = ps[:, GDN_CONV_CH + GDN_HD + GDN_HEADS:GDN_CONV_CH + GDN_HD + 2 * GDN_HEADS]
    g_s = -jnp.exp(l1_a_log) * jax.nn.softplus(a_s + l1_dt_bias)
    o_s, s_gdn = gdn_step(l2n(y_s[:, :GDN_HD]) * (GDN_D ** -0.5), l2n(y_s[:, GDN_HD:2 * GDN_HD]),
                          y_s[:, None, 2 * GDN_HD:], g_s[:, None, :], jax.nn.sigmoid(b_s)[:, None, :], state_gdn)
    og_s = _rmsnorm(o_s.reshape(db, GDN_HEADS, GDN_D), l1_g_o) * _silu(z_s.reshape(db, GDN_HEADS, GDN_D))
    og = jnp.concatenate([og_p, og_s.reshape(db, GDN_HD).astype(BF16)], axis=0)
    x, h = residual_modulate(x, t1, 2, mm(og, l1_w_out, tm=tm, name="out1"), l1_g_ffn, t1, 3, out_dtype=F32,
                             name="res_mix1_mod_ffn1", **row_kw)
    top_idx, top_w, rank, counts = router(h, l1_w_router, l1_b_router)
    src, slot, tile_expert, tile_valid = _route(top_idx, rank, counts, MOE_ROWS)
    take_rows = lambda a, idx: a.at[idx].get(mode="promise_in_bounds")
    ys = grouped_swiglu(h, l1_wg, l1_wu, l1_wd, tile_expert, tile_valid, tm=MOE_ROWS, tf=FF_COLS, name="moe",
                        src=src)
    y_out = residual_modulate(x, t1, 5, take_rows(ys, slot[0]), g_final, None, 0, out_dtype=F32,
                              name="res_moe_final_norm", a2=take_rows(ys, slot[1]), w=top_w, emit_x=False, **row_kw)

    def split(a, *tail):
        return a[:mp].reshape((nb, t) + tail), a[mp:].reshape((db, 1) + tail)

    y_prompt, y_sample = split(y_out, d)
    p_ckv, s_ckv = split(c_kv, nc)
    p_krope, s_krope = split(k_rope, nr)
    p_fk, s_fk = split(fk, FOX_KV_HEADS, FOX_DIM)
    p_fv, s_fv = split(fv, FOX_KV_HEADS, FOX_DIM)
    p_lf, s_lf = split(lf, FOX_HEADS)
    p_conv = jnp.stack([p1[(b + 1) * t - (GDN_CONV - 1):(b + 1) * t, :GDN_CONV_CH] for b in range(nb)])
    s_conv = xin_s[:, 1:]
    return (y_prompt, y_sample, p_ckv, p_krope, p_fk, p_fv, p_lf, p_gdn, p_conv,
            s_ckv, s_krope, s_fk, s_fv, s_lf, s_gdn, s_conv)
```

```python
import functools

import jax
import jax.numpy as jnp
import numpy as np
from jax import lax
from jax.experimental import pallas as pl
from jax.experimental.pallas import tpu as pltpu

D_MODEL = 2048
MLA_HEADS = 8
MLA_NOPE = 128
MLA_ROPE = 64
MLA_KV_LORA = 256
FOX_HEADS = 8
FOX_KV_HEADS = 2
FOX_DIM = 128
PAGE_SIZE = 128
MLA_SCALE = (MLA_NOPE + MLA_ROPE) ** -0.5
FOX_SCALE = FOX_DIM ** -0.5

BF16 = jnp.bfloat16
F32 = jnp.float32
VMEM_LIMIT = 56 * 1024 * 1024


def _cparams(*sem):
    return pltpu.CompilerParams(dimension_semantics=sem, vmem_limit_bytes=VMEM_LIMIT)


def _split3(x):
    hi = x.astype(BF16)
    r = x - hi.astype(F32)
    mid = r.astype(BF16)
    lo = (r - mid.astype(F32)).astype(BF16)
    return hi, mid, lo


DEC_PAGES_PER_STEP = 32
DEC_BATCH_PER_STEP = 1
N_CACHES = 5


def _decode_kernel(pt_ref, qa_ref, qr_ref, cn_ref, rn_ref, fq_ref, fkn_ref, fvn_ref, lfn_ref, tri_ref,
                   ckv_hbm, krt_hbm, fk_hbm, fv_hbm, lft_hbm, olat_ref, ofox_ref,
                   bc, br, bk, bv, bl, sems, m1, l1, a1, m2, l2, a2, csum, *, n_pages):
    pc, nbb = DEC_PAGES_PER_STEP, DEC_BATCH_PER_STEP
    nc = n_pages // pc
    total = (qa_ref.shape[0] // nbb) * nc
    caches = ((ckv_hbm, bc), (krt_hbm, br), (fk_hbm, bk), (fv_hbm, bv), (lft_hbm, bl))

    def page_copies(step, slot, for_wait):
        first_b = (step // nc) * nbb
        j = step % nc
        out = []
        for bb in range(nbb):
            for k in range(pc):
                page = 0 if for_wait else pt_ref[first_b + bb, n_pages - 1 - (j * pc + k)]
                for i, (src, dst) in enumerate(caches):
                    out.append(pltpu.make_async_copy(src.at[page], dst.at[slot, bb, k], sems.at[slot, i]))
        return out

    head = lax.broadcasted_iota(jnp.int32, (FOX_HEADS, 2 * PAGE_SIZE), 0)
    col = lax.broadcasted_iota(jnp.int32, (FOX_HEADS, 2 * PAGE_SIZE), 1)
    own = (col % FOX_KV_HEADS) == (head // (FOX_HEADS // FOX_KV_HEADS))
    g = FOX_HEADS // FOX_KV_HEADS

    def step_fn(step, carry):
        slot = step % 2
        first_b = (step // nc) * nbb
        j = step % nc

        @pl.when(step + 1 < total)
        def _():
            for c in page_copies(step + 1, 1 - slot, False):
                c.start()

        for c in page_copies(step, slot, True):
            c.wait()

        @pl.when(j == 0)
        def _():
            for bb in range(nbb):
                b = first_b + bb
                qa = qa_ref[b].astype(BF16).astype(F32)
                qr = qr_ref[b].astype(BF16).astype(F32)
                cn = cn_ref[b].astype(BF16).astype(F32)
                rn = rn_ref[b].astype(BF16).astype(F32)
                m1[bb] = (jnp.sum(qa * cn, axis=-1, keepdims=True)
                          + jnp.sum(qr * rn, axis=-1, keepdims=True)) * MLA_SCALE
                l1[bb] = jnp.ones((MLA_HEADS, 1), F32)
                a1[bb] = jnp.broadcast_to(cn, (MLA_HEADS, MLA_KV_LORA))
                fq = fq_ref[b].astype(BF16).astype(F32)
                fkn = fkn_ref[b].astype(BF16).astype(F32)
                fvn = fvn_ref[b].astype(BF16).astype(F32)
                kn_h = jnp.concatenate([jnp.broadcast_to(fkn[i:i + 1], (g, FOX_DIM)) for i in range(FOX_KV_HEADS)], 0)
                vn_h = jnp.concatenate([jnp.broadcast_to(fvn[i:i + 1], (g, FOX_DIM)) for i in range(FOX_KV_HEADS)], 0)
                m2[bb] = jnp.sum(fq * kn_h, axis=-1, keepdims=True) * FOX_SCALE
                l2[bb] = jnp.ones((FOX_HEADS, 1), F32)
                a2[bb] = vn_h
                csum[bb] = jnp.broadcast_to(lfn_ref[b], (FOX_HEADS, 2 * PAGE_SIZE))

        lf_rows = jnp.concatenate([bl[slot, bb, k] for bb in range(nbb) for k in range(pc)], axis=0)
        n_rows = lf_rows.shape[0]
        suf3 = jnp.dot(jnp.concatenate(_split3(lf_rows), axis=0), tri_ref[...], preferred_element_type=F32)
        suf = suf3[:n_rows] + suf3[n_rows:2 * n_rows] + suf3[2 * n_rows:]

        for bb in range(nbb):
            b = first_b + bb
            qa = qa_ref[b].astype(BF16)
            qr = qr_ref[b].astype(BF16)
            cbs = [bc[slot, bb, k].astype(BF16) for k in range(pc)]
            s_list = []
            for k in range(pc):
                rbt = br[slot, bb, k].astype(BF16)
                s = lax.dot_general(qa, cbs[k], (((1,), (1,)), ((), ())), preferred_element_type=F32)
                s = s + jnp.dot(qr, rbt, preferred_element_type=F32)
                s_list.append(s * MLA_SCALE)
            m_prev = m1[bb]
            m_new = m_prev
            for s in s_list:
                m_new = jnp.maximum(m_new, jnp.max(s, axis=-1, keepdims=True))
            alpha = jnp.exp(m_prev - m_new)
            l_new = l1[bb] * alpha
            acc = a1[bb] * alpha
            for k in range(pc):
                p = jnp.exp(s_list[k] - m_new)
                l_new = l_new + jnp.sum(p, axis=-1, keepdims=True)
                acc = acc + jnp.dot(p.astype(BF16), cbs[k], preferred_element_type=F32)
            m1[bb] = m_new
            l1[bb] = l_new
            a1[bb] = acc

            fq = fq_ref[b].astype(BF16)
            c_run = csum[bb]
            s_list = []
            vbs = []
            for k in range(pc):
                r0 = (bb * pc + k) * FOX_HEADS
                bias = c_run + suf[r0:r0 + FOX_HEADS, :2 * PAGE_SIZE]
                c_run = c_run + suf[r0:r0 + FOX_HEADS, 2 * PAGE_SIZE:]
                kb = bk[slot, bb, k].astype(BF16)
                vbs.append(bv[slot, bb, k].astype(BF16))
                s = lax.dot_general(fq, kb, (((1,), (1,)), ((), ())), preferred_element_type=F32) * FOX_SCALE + bias
                s_list.append(jnp.where(own, s, -jnp.inf))
            csum[bb] = c_run
            m_prev = m2[bb]
            m_new = m_prev
            for s in s_list:
                m_new = jnp.maximum(m_new, jnp.max(s, axis=-1, keepdims=True))
            alpha = jnp.exp(m_prev - m_new)
            l_new = l2[bb] * alpha
            acc = a2[bb] * alpha
            for k in range(pc):
                p = jnp.exp(s_list[k] - m_new)
                l_new = l_new + jnp.sum(p, axis=-1, keepdims=True)
                acc = acc + jnp.dot(p.astype(BF16), vbs[k], preferred_element_type=F32)
            m2[bb] = m_new
            l2[bb] = l_new
            a2[bb] = acc

        @pl.when(j == nc - 1)
        def _():
            for bb in range(nbb):
                olat_ref[first_b + bb] = a1[bb] / l1[bb]
                ofox_ref[first_b + bb] = a2[bb] / l2[bb]

        return carry

    for c in page_copies(0, 0, False):
        c.start()
    lax.fori_loop(0, total, step_fn, 0)


def _suffix_matrix():
    j = np.arange(PAGE_SIZE)[:, None]
    s = np.arange(2 * PAGE_SIZE)[None, :] // FOX_KV_HEADS
    strict = (j > s).astype(np.float32)
    return jnp.asarray(np.concatenate([strict, np.ones((PAGE_SIZE, 2 * PAGE_SIZE), np.float32)], axis=1), BF16)


def decode_attention(page_table, q_abs, q_rope, c_new, r_new, fq, fk_new, fv_new, lf_new,
                     cache_ckv, cache_krope, cache_fox_k, cache_fox_v, cache_logf):
    db, n_pages = page_table.shape
    pc, nbb = DEC_PAGES_PER_STEP, DEC_BATCH_PER_STEP
    assert db % nbb == 0 and n_pages % pc == 0
    n_pool = cache_ckv.shape[0]
    fk2 = cache_fox_k.reshape(n_pool, PAGE_SIZE * FOX_KV_HEADS, FOX_DIM)
    fv2 = cache_fox_v.reshape(n_pool, PAGE_SIZE * FOX_KV_HEADS, FOX_DIM)
    krope_t = jnp.swapaxes(cache_krope, 1, 2)
    logf_t = jnp.swapaxes(cache_logf, 1, 2)
    vmem = pl.BlockSpec(memory_space=pltpu.VMEM)
    hbm = pl.BlockSpec(memory_space=pl.ANY)
    slots = (2, nbb, pc)
    return pl.pallas_call(
        functools.partial(_decode_kernel, n_pages=n_pages),
        in_specs=[pl.BlockSpec(memory_space=pltpu.SMEM)] + [vmem] * 9 + [hbm] * N_CACHES,
        out_specs=[vmem, vmem],
        out_shape=[jax.ShapeDtypeStruct((db, MLA_HEADS, MLA_KV_LORA), F32),
                   jax.ShapeDtypeStruct((db, FOX_HEADS, FOX_DIM), F32)],
        scratch_shapes=[pltpu.VMEM(slots + (PAGE_SIZE, MLA_KV_LORA), F32),
                        pltpu.VMEM(slots + (MLA_ROPE, PAGE_SIZE), F32),
                        pltpu.VMEM(slots + (2 * PAGE_SIZE, FOX_DIM), F32),
                        pltpu.VMEM(slots + (2 * PAGE_SIZE, FOX_DIM), F32),
                        pltpu.VMEM(slots + (FOX_HEADS, PAGE_SIZE), F32),
                        pltpu.SemaphoreType.DMA((2, N_CACHES)),
                        pltpu.VMEM((nbb, MLA_HEADS, 1), F32), pltpu.VMEM((nbb, MLA_HEADS, 1), F32),
                        pltpu.VMEM((nbb, MLA_HEADS, MLA_KV_LORA), F32),
                        pltpu.VMEM((nbb, FOX_HEADS, 1), F32), pltpu.VMEM((nbb, FOX_HEADS, 1), F32),
                        pltpu.VMEM((nbb, FOX_HEADS, FOX_DIM), F32),
                        pltpu.VMEM((nbb, FOX_HEADS, 2 * PAGE_SIZE), F32)],
        compiler_params=pltpu.CompilerParams(vmem_limit_bytes=VMEM_LIMIT),
        name="decode_attention",
    )(page_table, q_abs, q_rope, c_new, r_new, fq, fk_new, fv_new, lf_new, _suffix_matrix(),
      cache_ckv, krope_t, fk2, fv2, logf_t)


def _matmul_kernel(x_ref, w_ref, *rest, silu_lhs, has_bias, w_nk):
    o_ref = rest[-1]
    x = x_ref[...]
    if silu_lhs:
        x = x.astype(F32)
        x = x * jax.nn.sigmoid(x)
    dims = (((1,), (1,)), ((), ())) if w_nk else (((1,), (0,)), ((), ()))
    acc = lax.dot_general(x.astype(BF16), w_ref[...].astype(BF16), dims, preferred_element_type=F32)
    if has_bias:
        acc = acc + rest[0][...]
    o_ref[...] = acc.astype(o_ref.dtype)


def matmul(x, w, *, tm, tn, name, out_dtype=F32, bias=None, silu_lhs=False, w_nk=False):
    m, k = x.shape
    n = w.shape[0] if w_nk else w.shape[1]
    tm = min(tm, m)
    tn = min(tn, n)
    w_spec = pl.BlockSpec((tn, k), lambda i, j: (j, 0)) if w_nk else pl.BlockSpec((k, tn), lambda i, j: (0, j))
    in_specs = [pl.BlockSpec((tm, k), lambda i, j: (i, 0)), w_spec]
    args = [x, w]
    if bias is not None:
        in_specs.append(pl.BlockSpec((1, tn), lambda i, j: (0, j)))
        args.append(bias.reshape(1, n))
    return pl.pallas_call(
        functools.partial(_matmul_kernel, silu_lhs=silu_lhs, has_bias=bias is not None, w_nk=w_nk),
        grid=(pl.cdiv(m, tm), pl.cdiv(n, tn)),
        in_specs=in_specs,
        out_specs=pl.BlockSpec((tm, tn), lambda i, j: (i, j)),
        out_shape=jax.ShapeDtypeStruct((m, n), out_dtype),
        compiler_params=_cparams("parallel", "arbitrary"),
        name=name,
    )(*args)


def _headwise_kernel(x_ref, w_ref, o_ref, *, transpose_w):
    x = x_ref[...].astype(BF16)
    w = w_ref[...].astype(BF16)
    dims = (((1,), (1,)), ((), ())) if transpose_w else (((1,), (0,)), ((), ()))
    o_ref[...] = lax.dot_general(x, w, dims, preferred_element_type=F32)


def headwise_matmul(x, w, *, heads, transpose_w, name):
    m = x.shape[0]
    dx = x.shape[1] // heads
    c = w.shape[0]
    dw = w.shape[1] // heads
    do = c if transpose_w else dw
    return pl.pallas_call(
        functools.partial(_headwise_kernel, transpose_w=transpose_w),
        grid=(heads,),
        in_specs=[pl.BlockSpec((m, dx), lambda h: (0, h)), pl.BlockSpec((c, dw), lambda h: (0, h))],
        out_specs=pl.BlockSpec((m, do), lambda h: (0, h)),
        out_shape=jax.ShapeDtypeStruct((m, heads * do), F32),
        compiler_params=_cparams("arbitrary"),
        name=name,
    )(x, w)


ROW_TILE = 128
EPS = 1e-6


def _table_row_block(i, tiles_per_prompt_batch, n_prompt_batches):
    return jnp.minimum(i // tiles_per_prompt_batch, n_prompt_batches)


def _modulate_kernel(x_ref, g_ref, *rest, modulated):
    o_ref = rest[-1]
    x = x_ref[...]
    y = x * lax.rsqrt(jnp.mean(x * x, axis=-1, keepdims=True) + EPS)
    y = y * g_ref[...]
    if modulated:
        shift_ref, scale_ref = rest[0], rest[1]
        y = y * (1.0 + scale_ref[...]) + shift_ref[...]
    o_ref[...] = y.astype(o_ref.dtype)


def modulate(x, gain, tables, k_shift, *, tiles_per_batch, n_batches, out_dtype, name):
    m, d = x.shape
    in_specs = [pl.BlockSpec((ROW_TILE, d), lambda i: (i, 0)), pl.BlockSpec((1, d), lambda i: (0, 0))]
    args = [x, gain.reshape(1, d)]
    if tables is not None:
        for k in (k_shift, k_shift + 1):
            in_specs.append(pl.BlockSpec(
                (None, ROW_TILE, d), lambda i, k=k: (k, _table_row_block(i, tiles_per_batch, n_batches), 0)))
            args.append(tables)
    return pl.pallas_call(
        functools.partial(_modulate_kernel, modulated=tables is not None),
        grid=(m // ROW_TILE,),
        in_specs=in_specs,
        out_specs=pl.BlockSpec((ROW_TILE, d), lambda i: (i, 0)),
        out_shape=jax.ShapeDtypeStruct((m, d), out_dtype),
        compiler_params=_cparams("parallel"),
        name=name,
    )(*args)


def _residual_modulate_kernel(x_ref, gate_ref, a_ref, *rest, two_terms, modulated, emit_x):
    rest = list(rest)
    a = a_ref[...]
    if two_terms:
        a2_ref, w_ref = rest[:2]
        rest = rest[2:]
        n = a.shape[0]
        diag = lax.broadcasted_iota(jnp.int32, (n, n), 0) == lax.broadcasted_iota(jnp.int32, (n, n), 1)
        w = w_ref[...]
        w1 = jnp.sum(jnp.where(diag, w[0:1, :], 0.0), axis=1, keepdims=True)
        w2 = jnp.sum(jnp.where(diag, w[1:2, :], 0.0), axis=1, keepdims=True)
        a = w1 * a + w2 * a2_ref[...]
    g_ref = rest.pop(0)
    if modulated:
        shift_ref, scale_ref = rest[:2]
        rest = rest[2:]
    x = x_ref[...] + gate_ref[...] * a
    if emit_x:
        rest.pop(0)[...] = x
    y = x * lax.rsqrt(jnp.mean(x * x, axis=-1, keepdims=True) + EPS)
    y = y * g_ref[...]
    if modulated:
        y = y * (1.0 + scale_ref[...]) + shift_ref[...]
    rest[0][...] = y.astype(rest[0].dtype)


def residual_modulate(x, gate_tables, k_gate, a, gain, mod_tables, k_shift, *, tiles_per_batch, n_batches,
                      out_dtype, name, a2=None, w=None, emit_x=True):
    m, d = x.shape
    row = pl.BlockSpec((ROW_TILE, d), lambda i: (i, 0))

    def table(k):
        return pl.BlockSpec((None, ROW_TILE, d), lambda i: (k, _table_row_block(i, tiles_per_batch, n_batches), 0))

    in_specs = [row, table(k_gate), row]
    args = [x, gate_tables, a]
    if a2 is not None:
        in_specs += [row, pl.BlockSpec((TOP_K, ROW_TILE), lambda i: (0, i))]
        args += [a2, w]
    in_specs.append(pl.BlockSpec((1, d), lambda i: (0, 0)))
    args.append(gain.reshape(1, d))
    if mod_tables is not None:
        in_specs += [table(k_shift), table(k_shift + 1)]
        args += [mod_tables, mod_tables]
    out_shape = [jax.ShapeDtypeStruct((m, d), out_dtype)]
    if emit_x:
        out_shape.insert(0, jax.ShapeDtypeStruct((m, d), F32))
    out = pl.pallas_call(
        functools.partial(_residual_modulate_kernel, two_terms=a2 is not None, modulated=mod_tables is not None,
                          emit_x=emit_x),
        grid=(m // ROW_TILE,),
        in_specs=in_specs,
        out_specs=[row] * len(out_shape),
        out_shape=out_shape,
        compiler_params=_cparams("parallel"),
        name=name,
    )(*args)
    return out if emit_x else out[0]


TOP_K = 2


def _router_kernel(h_ref, wrt_ref, b_ref, before_ref, idx_ref, w_ref, rank_ref, cnt_ref, carry):
    @pl.when(pl.program_id(0) == 0)
    def _():
        carry[...] = jnp.zeros_like(carry)

    logits = lax.dot_general(wrt_ref[...].astype(BF16), h_ref[...].astype(BF16), (((1,), (1,)), ((), ())),
                             preferred_element_type=F32) + b_ref[...]
    n_exp = logits.shape[0]
    e_iota = lax.broadcasted_iota(jnp.int32, logits.shape, 0)
    m1 = jnp.max(logits, axis=0, keepdims=True)
    i1 = jnp.min(jnp.where(logits == m1, e_iota, n_exp), axis=0, keepdims=True)
    rest = jnp.where(e_iota == i1, -jnp.inf, logits)
    m2 = jnp.max(rest, axis=0, keepdims=True)
    i2 = jnp.min(jnp.where(rest == m2, e_iota, n_exp), axis=0, keepdims=True)
    ex = jnp.exp(m2 - m1)
    w_ref[...] = jnp.concatenate([1.0 / (1.0 + ex), ex / (1.0 + ex)], axis=0)
    idx_ref[...] = jnp.concatenate([i1, i2], axis=0)
    oh1 = (e_iota == i1).astype(F32)
    oh2 = (e_iota == i2).astype(F32)
    both = oh1 + oh2
    earlier = carry[...] + jnp.dot(both.astype(BF16), before_ref[...], preferred_element_type=F32)
    rank_ref[...] = jnp.concatenate([jnp.sum(oh1 * earlier, axis=0, keepdims=True),
                                     jnp.sum(oh2 * earlier, axis=0, keepdims=True)], axis=0).astype(jnp.int32)
    carry[...] = carry[...] + jnp.sum(both, axis=1, keepdims=True)
    cnt_ref[...] = jnp.broadcast_to(carry[...], cnt_ref.shape).astype(jnp.int32)


def router(h, w_router, b_router):
    m, d = h.shape
    n_exp = w_router.shape[1]
    before = jnp.asarray(np.triu(np.ones((ROW_TILE, ROW_TILE), np.float32), 1), BF16)
    pair = pl.BlockSpec((TOP_K, ROW_TILE), lambda i: (0, i))
    idx, w, rank, cnt = pl.pallas_call(
        _router_kernel,
        grid=(m // ROW_TILE,),
        in_specs=[pl.BlockSpec((ROW_TILE, d), lambda i: (i, 0)), pl.BlockSpec((n_exp, d), lambda i: (0, 0)),
                  pl.BlockSpec((n_exp, 1), lambda i: (0, 0)), pl.BlockSpec((ROW_TILE, ROW_TILE), lambda i: (0, 0))],
        out_specs=[pair, pair, pair, pl.BlockSpec((n_exp, ROW_TILE), lambda i: (0, 0))],
        out_shape=[jax.ShapeDtypeStruct((TOP_K, m), jnp.int32), jax.ShapeDtypeStruct((TOP_K, m), F32),
                   jax.ShapeDtypeStruct((TOP_K, m), jnp.int32), jax.ShapeDtypeStruct((n_exp, ROW_TILE), jnp.int32)],
        scratch_shapes=[pltpu.VMEM((n_exp, 1), F32)],
        compiler_params=_cparams("arbitrary"),
        name="router",
    )(h, w_router.T, b_router.reshape(n_exp, 1), before)
    return idx, w, rank, cnt[:, 0]


GATHER_STEPS = 8


def _swiglu_kernel(te_ref, tv_ref, *rest, gather, tm, n_tiles):
    t = pl.program_id(0)
    f = pl.program_id(1)
    if gather:
        src_ref, x_hbm, wg_ref, wu_ref, wd_ref, o_ref, xg, xb, sem = rest
    else:
        x_ref, wg_ref, wu_ref, wd_ref, o_ref = rest

    @pl.when(f == 0)
    def _():
        o_ref[...] = jnp.zeros_like(o_ref)

    if gather:
        def row_copy(tok, r, slot):
            return pltpu.make_async_copy(x_hbm.at[pl.ds(tok, 1)], xg.at[slot, pl.ds(r, 1)], sem.at[slot])

        def issue(tile, r0, n):
            def body(i, carry):
                row_copy(src_ref[tile * tm + r0 + i], r0 + i, tile % 2).start()
                return carry
            lax.fori_loop(0, n, body, 0, unroll=8)

        @pl.when(jnp.logical_and(t == 0, f == 0))
        def _():
            issue(0, 0, tm)

        nxt = jnp.minimum(t + 1, n_tiles - 1)

        @pl.when(jnp.logical_and(jnp.logical_and(f < GATHER_STEPS, t + 1 < n_tiles), tv_ref[nxt] > 0))
        def _():
            issue(nxt, f * (tm // GATHER_STEPS), tm // GATHER_STEPS)

        @pl.when(jnp.logical_and(f == 0, tv_ref[t] > 0))
        def _():
            def body(i, carry):
                row_copy(0, i, t % 2).wait()
                return carry
            lax.fori_loop(0, tm, body, 0, unroll=8)
            xb[...] = xg[t % 2].astype(BF16)

    @pl.when(tv_ref[t] > 0)
    def _():
        x = xb[...] if gather else x_ref[...]
        g = jnp.dot(x, wg_ref[...].astype(BF16), preferred_element_type=F32)
        u = jnp.dot(x, wu_ref[...].astype(BF16), preferred_element_type=F32)
        a = (g * jax.nn.sigmoid(g) * u).astype(BF16)
        o_ref[...] += jnp.dot(a, wd_ref[...].astype(BF16), preferred_element_type=F32)


def grouped_swiglu(x, wg, wu, wd, tile_expert, tile_valid, *, tm, tf, name, src=None):
    d = x.shape[1]
    gather = src is not None
    n_rows = src.shape[0] if gather else x.shape[0]
    n_tiles = n_rows // tm
    nf = wg.shape[2] // tf
    assert tm % GATHER_STEPS == 0 and nf >= GATHER_STEPS

    def f_eff(t, f, tv):
        return jnp.where(tv[t] > 0, f, nf - 1)

    w_specs = [pl.BlockSpec((None, d, tf), lambda t, f, te, tv, *_: (te[t], 0, f_eff(t, f, tv))),
               pl.BlockSpec((None, d, tf), lambda t, f, te, tv, *_: (te[t], 0, f_eff(t, f, tv))),
               pl.BlockSpec((None, tf, d), lambda t, f, te, tv, *_: (te[t], f_eff(t, f, tv), 0))]
    if gather:
        x_spec = pl.BlockSpec(memory_space=pl.ANY)
        scratch = [pltpu.VMEM((2, tm, d), F32), pltpu.VMEM((tm, d), BF16), pltpu.SemaphoreType.DMA((2,))]
        prefetch = (tile_expert, tile_valid, src)
    else:
        x_spec = pl.BlockSpec((tm, d), lambda t, f, te, tv: (t, 0))
        scratch = []
        prefetch = (tile_expert, tile_valid)
    grid_spec = pltpu.PrefetchScalarGridSpec(
        num_scalar_prefetch=len(prefetch),
        grid=(n_tiles, nf),
        in_specs=[x_spec] + w_specs,
        out_specs=pl.BlockSpec((tm, d), lambda t, f, te, tv, *_: (t, 0)),
        scratch_shapes=scratch,
    )
    return pl.pallas_call(
        functools.partial(_swiglu_kernel, gather=gather, tm=tm, n_tiles=n_tiles),
        grid_spec=grid_spec,
        out_shape=jax.ShapeDtypeStruct((n_rows, d), F32),
        compiler_params=_cparams("arbitrary", "arbitrary"),
        name=name,
    )(*prefetch, x, wg, wu, wd)


def _flash_kernel(q_ref, k_ref, v_ref, *rest, scale, use_bias):
    if use_bias:
        fq_ref, fk_ref = rest[0], rest[1]
        rest = rest[2:]
    o_ref, m_s, l_s, acc_s = rest
    qi = pl.program_id(2)
    ki = pl.program_id(3)

    @pl.when(ki == 0)
    def _():
        m_s[...] = jnp.full_like(m_s, -jnp.inf)
        l_s[...] = jnp.zeros_like(l_s)
        acc_s[...] = jnp.zeros_like(acc_s)

    @pl.when(ki <= qi)
    def _():
        q = q_ref[...]
        k = k_ref[...]
        s = lax.dot_general(q, k, (((1,), (1,)), ((), ())), preferred_element_type=F32) * scale
        if use_bias:
            s = s + fq_ref[...] - fk_ref[...]
        row = lax.broadcasted_iota(jnp.int32, s.shape, 0)
        col = lax.broadcasted_iota(jnp.int32, s.shape, 1)
        s = jnp.where(jnp.logical_and(ki == qi, col > row), -jnp.inf, s)
        m_prev = m_s[...]
        m_new = jnp.maximum(m_prev, jnp.max(s, axis=-1, keepdims=True))
        alpha = jnp.exp(m_prev - m_new)
        p = jnp.exp(s - m_new)
        l_s[...] = l_s[...] * alpha + jnp.sum(p, axis=-1, keepdims=True)
        acc_s[...] = acc_s[...] * alpha + jnp.dot(p.astype(BF16), v_ref[...], preferred_element_type=F32)
        m_s[...] = m_new

    @pl.when(ki == qi)
    def _():
        o_ref[...] = (acc_s[...] / l_s[...]).astype(o_ref.dtype)


def flash_attention(q, k, v, *, scale, tile, name, fq=None, fk=None):
    b, h, t, dk = q.shape
    hk = k.shape[1]
    dv = v.shape[3]
    g = h // hk
    nt = t // tile
    in_specs = [pl.BlockSpec((None, None, tile, dk), lambda b_, h_, i, j: (b_, h_, i, 0)),
                pl.BlockSpec((None, None, tile, dk), lambda b_, h_, i, j: (b_, h_ // g, jnp.minimum(i, j), 0)),
                pl.BlockSpec((None, None, tile, dv), lambda b_, h_, i, j: (b_, h_ // g, jnp.minimum(i, j), 0))]
    args = [q, k, v]
    if fq is not None:
        in_specs += [pl.BlockSpec((None, None, tile, 1), lambda b_, h_, i, j: (b_, h_, i, 0)),
                     pl.BlockSpec((None, None, 1, tile), lambda b_, h_, i, j: (b_, h_, 0, jnp.minimum(i, j)))]
        args += [fq, fk]
    return pl.pallas_call(
        functools.partial(_flash_kernel, scale=scale, use_bias=fq is not None),
        grid=(b, h, nt, nt),
        in_specs=in_specs,
        out_specs=pl.BlockSpec((None, tile, dv), lambda b_, h_, i, j: (b_, i, h_)),
        out_shape=jax.ShapeDtypeStruct((b, t, h * dv), BF16),
        scratch_shapes=[pltpu.VMEM((tile, 1), F32), pltpu.VMEM((tile, 1), F32), pltpu.VMEM((tile, dv), F32)],
        compiler_params=_cparams("parallel", "parallel", "arbitrary", "arbitrary"),
        name=name,
    )(*args)


GDN_HEADS = 16
GDN_D = 128
GDN_CHUNK = 64
GDN_CONV = 4
GDN_HD = GDN_HEADS * GDN_D
GDN_CONV_CH = 3 * GDN_HD


def _dot_hi(a, b):
    a_hi = a.astype(BF16)
    a_lo = (a - a_hi.astype(F32)).astype(BF16)
    b_hi = b.astype(BF16)
    b_lo = (b - b_hi.astype(F32)).astype(BF16)
    return (jnp.dot(a_hi, b_hi, preferred_element_type=F32) + jnp.dot(a_hi, b_lo, preferred_element_type=F32)
            + jnp.dot(a_lo, b_hi, preferred_element_type=F32))


def _silu(x):
    return x * jax.nn.sigmoid(x)


def _gdn_prompt_kernel(qkv_ref, z_ref, ab_ref, cw_ref, alog_ref, dtb_ref, go_ref, lincl_ref,
                       og_ref, sfin_ref, state, xprev):
    n = pl.program_id(1)
    c = GDN_CHUNK

    @pl.when(n == 0)
    def _():
        state[...] = jnp.zeros_like(state)
        xprev[...] = jnp.zeros_like(xprev)

    x = qkv_ref[...]
    xp = xprev[...]
    trow = lax.broadcasted_iota(jnp.int32, (c, 1), 0)
    cw = cw_ref[...]
    y = x * cw[GDN_CONV - 1:GDN_CONV]
    for s in range(1, GDN_CONV):
        shifted = pltpu.roll(jnp.where(trow >= c - s, xp, x), s, axis=0)
        y = y + shifted * cw[GDN_CONV - 1 - s:GDN_CONV - s]
    xprev[...] = x
    y = _silu(y)

    ab = ab_ref[...]
    a_in = ab[:, :GDN_HEADS] + dtb_ref[...]
    softplus = jnp.maximum(a_in, 0.0) + jnp.log1p(jnp.exp(-jnp.abs(a_in)))
    gate = -jnp.exp(alog_ref[...]) * softplus
    beta = jax.nn.sigmoid(ab[:, GDN_HEADS:2 * GDN_HEADS])

    lincl = lincl_ref[...]
    g3 = _split3(gate)
    gc_cols = sum(jnp.dot(lincl, t, preferred_element_type=F32) for t in g3)
    gc_rows = sum(lax.dot_general(t, lincl, (((0,), (1,)), ((), ())), preferred_element_type=F32) for t in g3)
    row = lax.broadcasted_iota(jnp.int32, (c, c), 0)
    col = lax.broadcasted_iota(jnp.int32, (c, c), 1)
    incl = row >= col
    strict = row > col
    eye = (row == col).astype(F32)
    for h0 in range(0, GDN_HEADS, GDN_HEAD_GROUP):
        _gdn_head_group(range(h0, h0 + GDN_HEAD_GROUP), y, gc_cols, gc_rows, beta, incl, strict, eye,
                        z_ref, go_ref, og_ref, state)

    @pl.when(n == pl.num_programs(1) - 1)
    def _():
        sfin_ref[...] = state[...]


GDN_HEAD_GROUP = 8


def _gdn_head_group(heads, y, gc_cols, gc_rows, beta, incl, strict, eye, z_ref, go_ref, og_ref, state):
    c = GDN_CHUNK
    qbs, kbs, khs, vhs, decays, gccs, bcols, zmats = {}, {}, {}, {}, {}, {}, {}, {}
    for h in heads:
        qh = y[:, h * GDN_D:(h + 1) * GDN_D]
        kh = y[:, GDN_HD + h * GDN_D:GDN_HD + (h + 1) * GDN_D]
        qh = qh * lax.rsqrt(jnp.sum(qh * qh, axis=-1, keepdims=True) + EPS) * (GDN_D ** -0.5)
        kh = kh * lax.rsqrt(jnp.sum(kh * kh, axis=-1, keepdims=True) + EPS)
        gcc = gc_cols[:, h:h + 1]
        bcol = beta[:, h:h + 1]
        decay = jnp.exp(jnp.where(incl, gcc - gc_rows[h:h + 1, :], -jnp.inf))
        kb = kh.astype(BF16)
        kk = lax.dot_general(kb, kb, (((1,), (1,)), ((), ())), preferred_element_type=F32)
        nmat = -(jnp.where(strict, decay, 0.0) * bcol * kk)
        zmats[h] = (_dot_hi(nmat, nmat), eye + nmat)
        qbs[h], kbs[h], khs[h], decays[h], gccs[h], bcols[h] = qh.astype(BF16), kb, kh, decay, gcc, bcol
        vhs[h] = y[:, 2 * GDN_HD + h * GDN_D:2 * GDN_HD + (h + 1) * GDN_D]
    for _ in range(4):
        zmats = {h: (_dot_hi(nm, nm), sm + _dot_hi(nm, sm)) for h, (nm, sm) in zmats.items()}
    wus = {}
    for h in heads:
        nm, sm = zmats[h]
        tinv = sm + _dot_hi(nm, sm)
        rhs = jnp.concatenate([(bcols[h] * jnp.exp(gccs[h])) * khs[h], bcols[h] * vhs[h]], axis=1)
        wus[h] = _dot_hi(tinv, rhs)
    for h in heads:
        sl = slice(h * GDN_D, (h + 1) * GDN_D)
        gcc = gccs[h]
        w, u = wus[h][:, :GDN_D], wus[h][:, GDN_D:]
        qk = lax.dot_general(qbs[h], kbs[h], (((1,), (1,)), ((), ())), preferred_element_type=F32) * decays[h]
        s_old = state[h]
        sb = s_old.astype(BF16)
        wq_s = jnp.dot(jnp.concatenate([w.astype(BF16), qbs[h]], axis=0), sb, preferred_element_type=F32)
        ub = (u - wq_s[:c]).astype(BF16)
        o = jnp.exp(gcc) * wq_s[c:] + jnp.dot(qk.astype(BF16), ub, preferred_element_type=F32)
        glast = gcc[c - 1:c, :]
        k_end = (khs[h] * jnp.exp(glast - gcc)).astype(BF16)
        state[h] = jnp.exp(glast) * s_old + lax.dot_general(k_end, ub, (((0,), (0,)), ((), ())),
                                                            preferred_element_type=F32)
        o = o * lax.rsqrt(jnp.mean(o * o, axis=-1, keepdims=True) + EPS) * go_ref[...]
        og_ref[:, sl] = (o * _silu(z_ref[:, sl])).astype(og_ref.dtype)


def gdn_prompt(p1, nb, t, conv_w, a_log, dt_bias, g_o):
    c = GDN_CHUNK
    nchunk = t // c
    lincl = jnp.asarray(np.tril(np.ones((c, c), np.float32)), BF16)
    row_block = lambda b_, n: b_ * nchunk + n
    gate_col = (GDN_CONV_CH + GDN_HD) // 128
    small = lambda shape: pl.BlockSpec(shape, lambda b_, n: (0, 0))
    return pl.pallas_call(
        _gdn_prompt_kernel,
        grid=(nb, nchunk),
        in_specs=[pl.BlockSpec((c, GDN_CONV_CH), lambda b_, n: (row_block(b_, n), 0)),
                  pl.BlockSpec((c, GDN_HD), lambda b_, n: (row_block(b_, n), GDN_CONV_CH // GDN_HD)),
                  pl.BlockSpec((c, 128), lambda b_, n: (row_block(b_, n), gate_col)),
                  small((GDN_CONV, GDN_CONV_CH)), small((1, GDN_HEADS)), small((1, GDN_HEADS)), small((1, GDN_D)),
                  small((c, c))],
        out_specs=[pl.BlockSpec((c, GDN_HD), lambda b_, n: (row_block(b_, n), 0)),
                   pl.BlockSpec((None, GDN_HEADS, GDN_D, GDN_D), lambda b_, n: (b_, 0, 0, 0))],
        out_shape=[jax.ShapeDtypeStruct((nb * t, GDN_HD), BF16),
                   jax.ShapeDtypeStruct((nb, GDN_HEADS, GDN_D, GDN_D), F32)],
        scratch_shapes=[pltpu.VMEM((GDN_HEADS, GDN_D, GDN_D), F32), pltpu.VMEM((c, GDN_CONV_CH), F32)],
        compiler_params=_cparams("parallel", "arbitrary"),
        name="gdn_prompt",
    )(p1, p1, p1, conv_w.reshape(GDN_CONV, GDN_CONV_CH), a_log.reshape(1, GDN_HEADS),
      dt_bias.reshape(1, GDN_HEADS), g_o.reshape(1, GDN_D), lincl)


def _gdn_step_kernel(q_ref, k_ref, v_ref, g_ref, beta_ref, s_ref, o_ref, snew_ref):
    eg = jnp.exp(g_ref[...])
    diag = (lax.broadcasted_iota(jnp.int32, (GDN_D, GDN_D), 0) == lax.broadcasted_iota(jnp.int32, (GDN_D, GDN_D), 1))
    for h in range(GDN_HEADS):
        sl = slice(h * GDN_D, (h + 1) * GDN_D)
        s1 = s_ref[h] * eg[:, h:h + 1]
        k_row = k_ref[:, sl]
        k8 = jnp.broadcast_to(k_row, (8, GDN_D)).astype(BF16)
        ks = jnp.dot(k8, s1.astype(BF16), preferred_element_type=F32)[0:1]
        u = beta_ref[:, h:h + 1] * (v_ref[:, sl] - ks)
        k_col = jnp.sum(jnp.where(diag, k_row, 0.0), axis=1, keepdims=True)
        s2 = s1 + k_col * u
        q8 = jnp.broadcast_to(q_ref[:, sl], (8, GDN_D)).astype(BF16)
        o_ref[:, sl] = jnp.dot(q8, s2.astype(BF16), preferred_element_type=F32)[0:1]
        snew_ref[h] = s2


def gdn_step(q, k, v, g, beta, s0):
    b = q.shape[0]
    vec = pl.BlockSpec((None, 1, GDN_HD), lambda i: (i, 0, 0))
    gate = pl.BlockSpec((None, 1, GDN_HEADS), lambda i: (i, 0, 0))
    st = pl.BlockSpec((None, GDN_HEADS, GDN_D, GDN_D), lambda i: (i, 0, 0, 0))
    return pl.pallas_call(
        _gdn_step_kernel,
        grid=(b,),
        in_specs=[vec, vec, vec, gate, gate, st],
        out_specs=[vec, st],
        out_shape=[jax.ShapeDtypeStruct((b, 1, GDN_HD), F32), jax.ShapeDtypeStruct(s0.shape, F32)],
        compiler_params=_cparams("parallel"),
        name="gdn_step",
    )(q, k, v, g, beta, s0)


ROPE_THETA = 10000.0
MLA_Q_LORA = 512
N_MOD = 6
MATMUL_ROWS = 1040
MATMUL_COLS = 512
MOE_ROWS = 1024
FF_COLS = 256
FLASH_TILE = 1024


def _pick_tile(m, target):
    best = None
    for t in range(16, min(m, target) + 1, 16):
        if m % t == 0:
            best = t
    return best if best is not None else m


def _rmsnorm(x, gain):
    y = x * lax.rsqrt(jnp.mean(x * x, axis=-1, keepdims=True) + EPS)
    return y * gain


def _rope(x, cos, sin):
    half = x.shape[-1] // 2
    x1, x2 = x[..., :half], x[..., half:]
    c, s = cos[:, None, :], sin[:, None, :]
    return jnp.concatenate([x1 * c - x2 * s, x1 * s + x2 * c], axis=-1)


def _route(idx, rank, counts, tm):
    n_experts = counts.shape[0]
    m = idx.shape[1]
    tiles_e = (counts + tm - 1) // tm
    tile_end = jnp.cumsum(tiles_e)
    tile_start = tile_end - tiles_e
    first_row = sum(jnp.where(idx == e, tile_start[e], 0) for e in range(n_experts)) * tm
    pos = first_row + rank
    n_tiles = (TOP_K * m + n_experts * (tm - 1)) // tm
    tok = jnp.broadcast_to(jnp.arange(m, dtype=jnp.int32)[None, :], (TOP_K, m))
    src = jnp.zeros((n_tiles * tm,), jnp.int32).at[pos.reshape(-1)].set(tok.reshape(-1))
    t = jnp.arange(n_tiles, dtype=jnp.int32)
    tile_valid = (t < tile_end[-1]).astype(jnp.int32)
    t_clamped = jnp.minimum(t, tile_end[-1] - 1)
    tile_expert = jnp.sum((t_clamped[:, None] >= tile_end[None, :]).astype(jnp.int32), axis=1)
    return src, pos, tile_expert.astype(jnp.int32), tile_valid


def kernel(x_prompt, x_sample, cache_mla_ckv, cache_mla_krope, cache_fox_k, cache_fox_v, cache_fox_logf, state_gdn, state_gdn_conv, page_table, c_prompt, c_sample, l0_w_mod, l0_b_mod, l0_g_mix, l0_g_ffn, l0_w_in, l0_g_qa, l0_w_uq, l0_g_kva, l0_w_uk, l0_w_uv, l0_b_f, l0_w_out, l0_w1, l0_w3, l0_w2, l1_w_mod, l1_b_mod, l1_g_mix, l1_g_ffn, l1_w_in, l1_conv_w, l1_a_log, l1_dt_bias, l1_g_o, l1_w_out, l1_w_router, l1_b_router, l1_wg, l1_wu, l1_wd, g_final):
    nb, t, d = x_prompt.shape
    db = x_sample.shape[0]
    assert x_sample.shape[1] == 1 and db == ROW_TILE and t % ROW_TILE == 0
    mp = nb * t
    m = mp + db
    past = page_table.shape[1] * cache_mla_ckv.shape[1]
    tpb = t // ROW_TILE
    tm = _pick_tile(m, MATMUL_ROWS)
    tmp = _pick_tile(mp, MATMUL_ROWS)
    mm = functools.partial(matmul, tn=MATMUL_COLS)
    row_kw = dict(tiles_per_batch=tpb, n_batches=nb)

    x = jnp.concatenate([x_prompt.reshape(mp, d), x_sample.reshape(db, d)], axis=0)
    c_all = jnp.concatenate([c_prompt, c_sample], axis=0)

    def mod_tables(w_mod, b_mod, name):
        mod = matmul(c_all, w_mod, tm=nb + db, tn=1024, bias=b_mod, silu_lhs=True, name=name)
        mod = jnp.swapaxes(mod.reshape(nb + db, N_MOD, d), 0, 1)
        return jnp.concatenate([jnp.repeat(mod[:, :nb], ROW_TILE, axis=1), mod[:, nb:]], axis=1)

    pos = jnp.concatenate([jnp.tile(jnp.arange(t, dtype=jnp.int32), nb), jnp.full((db,), past, jnp.int32)])
    half = MLA_ROPE // 2
    inv = ROPE_THETA ** (-jnp.arange(half, dtype=F32) / half)
    ang = pos.astype(F32)[:, None] * inv[None, :]
    cos, sin = jnp.cos(ang), jnp.sin(ang)

    t0 = mod_tables(l0_w_mod, l0_b_mod, "adaln0")
    h = modulate(x, l0_g_mix, t0, 0, out_dtype=BF16, name="mod_mix0", **row_kw)
    p0 = mm(h, l0_w_in.T, tm=tm, name="proj0", w_nk=True)
    nq, nc, nr = MLA_Q_LORA, MLA_KV_LORA, MLA_ROPE
    nfq, nfk = FOX_HEADS * FOX_DIM, FOX_KV_HEADS * FOX_DIM
    offs = np.cumsum([0, nq, nc, nr, nfq, nfk, nfk, FOX_HEADS])
    q_lat, kv_lat, k_r, fq, fk, fv, f_logit = (p0[:, offs[i]:offs[i + 1]] for i in range(7))
    w_uq = l0_w_uq.reshape(nq, MLA_HEADS, MLA_NOPE + MLA_ROPE)
    w_q = jnp.concatenate([w_uq[:, :, :MLA_NOPE].reshape(nq, -1), w_uq[:, :, MLA_NOPE:].reshape(nq, -1)], axis=1)
    qq = mm(_rmsnorm(q_lat, l0_g_qa).astype(BF16), w_q, tm=tm, name="q_up")
    q_nope = qq[:, :MLA_HEADS * MLA_NOPE]
    q_rope = _rope(qq[:, MLA_HEADS * MLA_NOPE:].reshape(m, MLA_HEADS, MLA_ROPE), cos, sin)
    c_kv = _rmsnorm(kv_lat, l0_g_kva)
    k_rope = _rope(k_r[:, None, :], cos, sin)[:, 0]
    lf = jax.nn.log_sigmoid(f_logit + l0_b_f)

    w_uk2 = l0_w_uk.reshape(nc, MLA_HEADS * MLA_NOPE)
    w_uv2 = l0_w_uv.reshape(nc, MLA_HEADS * MLA_NOPE)
    kv = mm(c_kv[:mp].astype(BF16), jnp.concatenate([w_uk2, w_uv2], axis=1), tm=tmp, name="kv_up", out_dtype=BF16)

    def heads_first(a, n_heads):
        return jnp.swapaxes(a.reshape(nb, t, n_heads, -1), 1, 2).astype(BF16)

    q_cat = jnp.concatenate([q_nope[:mp].reshape(nb, t, MLA_HEADS, MLA_NOPE),
                             q_rope[:mp].reshape(nb, t, MLA_HEADS, MLA_ROPE)], axis=-1)
    k_cat = jnp.concatenate([kv[:, :MLA_HEADS * MLA_NOPE].reshape(nb, t, MLA_HEADS, MLA_NOPE),
                             jnp.broadcast_to(k_rope[:mp].reshape(nb, t, 1, MLA_ROPE), (nb, t, MLA_HEADS, MLA_ROPE))],
                            axis=-1)
    ftile = _pick_tile(t, FLASH_TILE)
    o_mla_p = flash_attention(heads_first(q_cat, MLA_HEADS), heads_first(k_cat, MLA_HEADS),
                              heads_first(kv[:, MLA_HEADS * MLA_NOPE:], MLA_HEADS),
                              scale=MLA_SCALE, tile=ftile, name="flash_mla")
    fcum = jnp.swapaxes(jnp.cumsum(lf[:mp].reshape(nb, t, FOX_HEADS), axis=1), 1, 2)
    o_fox_p = flash_attention(heads_first(fq[:mp], FOX_HEADS), heads_first(fk[:mp], FOX_KV_HEADS),
                              heads_first(fv[:mp], FOX_KV_HEADS), scale=FOX_SCALE, tile=ftile, name="flash_fox",
                              fq=fcum[..., None], fk=fcum[:, :, None, :])

    q_abs = headwise_matmul(q_nope[mp:], w_uk2, heads=MLA_HEADS, transpose_w=True, name="q_absorb")
    o_lat, o_fox_s = decode_attention(
        page_table, q_abs.reshape(db, MLA_HEADS, nc), q_rope[mp:], c_kv[mp:, None, :], k_rope[mp:, None, :],
        fq[mp:].reshape(db, FOX_HEADS, FOX_DIM), fk[mp:].reshape(db, FOX_KV_HEADS, FOX_DIM),
        fv[mp:].reshape(db, FOX_KV_HEADS, FOX_DIM), lf[mp:, :, None],
        cache_mla_ckv, cache_mla_krope, cache_fox_k, cache_fox_v, cache_fox_logf)
    o_mla_s = headwise_matmul(o_lat.reshape(db, MLA_HEADS * nc), w_uv2, heads=MLA_HEADS, transpose_w=False,
                              name="o_unabsorb")

    o = jnp.concatenate([jnp.concatenate([o_mla_p, o_fox_p], axis=-1).reshape(mp, -1),
                         jnp.concatenate([o_mla_s, o_fox_s.reshape(db, -1)], axis=-1).astype(BF16)], axis=0)
    x, h = residual_modulate(x, t0, 2, mm(o, l0_w_out, tm=tm, name="out0"), l0_g_ffn, t0, 3, out_dtype=BF16,
                             name="res_mix0_mod_ffn0", **row_kw)
    n_row_tiles = m // tm
    ffn = grouped_swiglu(h, l0_w1[None], l0_w3[None], l0_w2[None], jnp.zeros((n_row_tiles,), jnp.int32),
                         jnp.ones((n_row_tiles,), jnp.int32), tm=tm, tf=FF_COLS, name="ffn0")

    t1 = mod_tables(l1_w_mod, l1_b_mod, "adaln1")
    x, h = residual_modulate(x, t0, 5, ffn, l1_g_mix, t1, 0, out_dtype=BF16, name="res_ffn0_mod_mix1", **row_kw)
    p1 = mm(h, l1_w_in.T, tm=tm, name="proj1", w_nk=True)
    og_p, p_gdn = gdn_prompt(p1, nb, t, l1_conv_w, l1_a_log, l1_dt_bias, l1_g_o)
    ps = p1[mp:]
    cw = l1_conv_w.reshape(GDN_CONV, GDN_CONV_CH)
    xin_s = jnp.concatenate([state_gdn_conv, ps[:, None, :GDN_CONV_CH]], axis=1)
    y_s = _silu(sum(xin_s[:, w] * cw[w] for w in range(GDN_CONV)))

    def l2n(a):
        a = a.reshape(db, GDN_HEADS, GDN_D)
        return (a * lax.rsqrt(jnp.sum(a * a, axis=-1, keepdims=True) + EPS)).reshape(db, 1, GDN_HD)

    z_s = ps[:, GDN_CONV_CH:GDN_CONV_CH + GDN_HD]
    a_s = ps[:, GDN_CONV_CH + GDN_HD:GDN_CONV_CH + GDN_HD + GDN_HEADS]
    b_s = ps[:, GDN_CONV_CH + GDN_HD + GDN_HEADS:GDN_CONV_CH + GDN_HD + 2 * GDN_HEADS]
    g_s = -jnp.exp(l1_a_log) * jax.nn.softplus(a_s + l1_dt_bias)
    o_s, s_gdn = gdn_step(l2n(y_s[:, :GDN_HD]) * (GDN_D ** -0.5), l2n(y_s[:, GDN_HD:2 * GDN_HD]),
                          y_s[:, None, 2 * GDN_HD:], g_s[:, None, :], jax.nn.sigmoid(b_s)[:, None, :], state_gdn)
    og_s = _rmsnorm(o_s.reshape(db, GDN_HEADS, GDN_D), l1_g_o) * _silu(z_s.reshape(db, GDN_HEADS, GDN_D))
    og = jnp.concatenate([og_p, og_s.reshape(db, GDN_HD).astype(BF16)], axis=0)
    x, h = residual_modulate(x, t1, 2, mm(og, l1_w_out, tm=tm, name="out1"), l1_g_ffn, t1, 3, out_dtype=F32,
                             name="res_mix1_mod_ffn1", **row_kw)
    top_idx, top_w, rank, counts = router(h, l1_w_router, l1_b_router)
    src, slot, tile_expert, tile_valid = _route(top_idx, rank, counts, MOE_ROWS)
    take_rows = lambda a, idx: a.at[idx].get(mode="promise_in_bounds")
    ys = grouped_swiglu(h, l1_wg, l1_wu, l1_wd, tile_expert, tile_valid, tm=MOE_ROWS, tf=FF_COLS, name="moe",
                        src=src)
    y_out = residual_modulate(x, t1, 5, take_rows(ys, slot[0]), g_final, None, 0, out_dtype=F32,
                              name="res_moe_final_norm", a2=take_rows(ys, slot[1]), w=top_w, emit_x=False, **row_kw)

    def split(a, *tail):
        return a[:mp].reshape((nb, t) + tail), a[mp:].reshape((db, 1) + tail)

    y_prompt, y_sample = split(y_out, d)
    p_ckv, s_ckv = split(c_kv, nc)
    p_krope, s_krope = split(k_rope, nr)
    p_fk, s_fk = split(fk, FOX_KV_HEADS, FOX_DIM)
    p_fv, s_fv = split(fv, FOX_KV_HEADS, FOX_DIM)
    p_lf, s_lf = split(lf, FOX_HEADS)
    p_conv = jnp.stack([p1[(b + 1) * t - (GDN_CONV - 1):(b + 1) * t, :GDN_CONV_CH] for b in range(nb)])
    s_conv = xin_s[:, 1:]
    return (y_prompt, y_sample, p_ckv, p_krope, p_fk, p_fv, p_lf, p_gdn, p_conv,
            s_ckv, s_krope, s_fk, s_fv, s_lf, s_gdn, s_conv)
```
